```python
import math
import jax, jax.numpy as jnp
from jax import lax
import numpy as np

D_MODEL = 2048
BATCH = 2
SEQ = 4096
DEPTH = 2

HEAD_DIM = 128
N_HEADS = D_MODEL // HEAD_DIM
H_DIFF = N_HEADS // 4
H_FOX = (N_HEADS - H_DIFF) // 2
H_DIL = N_HEADS - H_FOX - H_DIFF
DIFF_QK_DIM = HEAD_DIM // 2
DIFF_V_DIM = HEAD_DIM
DIL_PATTERNS = ((128, 1), (512, 4), (2048, 16))
Q_BLOCK = 128
NUM_BUCKETS = 32
MAX_DISTANCE = 2048
D_FF = 4 * D_MODEL
N_MOD = 6
RMS_EPS = 1e-6

FOX_QKV = 3 * H_FOX * HEAD_DIM
DIL_QKV = 3 * H_DIL * HEAD_DIM
DIFF_QK = 4 * H_DIFF * DIFF_QK_DIM
DIFF_V = H_DIFF * DIFF_V_DIM
P_IN = FOX_QKV + H_FOX + DIL_QKV + DIFF_QK + DIFF_V
SPLIT_IDX = [FOX_QKV, FOX_QKV + H_FOX, FOX_QKV + H_FOX + DIL_QKV, FOX_QKV + H_FOX + DIL_QKV + DIFF_QK]

kernel_name = "hybrid_fox_dilated_diff_adaln"


def rmsnorm(x, g):
    x32 = x.astype(jnp.float32)
    y = x32 * lax.rsqrt(jnp.mean(x32 * x32, axis=-1, keepdims=True) + RMS_EPS)
    return (y * g.astype(jnp.float32)).astype(x.dtype)


def t5_bucket(dist):
    max_exact = NUM_BUCKETS // 2
    n = jnp.maximum(dist, 0)
    nf = jnp.maximum(n, 1).astype(jnp.float32)
    large = max_exact + (jnp.log(nf / max_exact) / math.log(MAX_DISTANCE / max_exact)
                         * (NUM_BUCKETS - max_exact)).astype(jnp.int32)
    large = jnp.minimum(large, NUM_BUCKETS - 1)
    return jnp.where(n < max_exact, n, large)


def to_heads(a, h):
    b, s, _ = a.shape
    return a.reshape(b, s, h, -1).transpose(0, 2, 1, 3)


def from_heads(a):
    b, h, s, dh = a.shape
    return a.transpose(0, 2, 1, 3).reshape(b, s, h * dh)


def fox_attention(q, k, v, log_f):
    B, H, S, dh = q.shape
    nb = S // Q_BLOCK
    scale = dh ** -0.5
    F = jnp.cumsum(log_f.astype(jnp.float32), axis=-1)
    qb = q.reshape(B, H, nb, Q_BLOCK, dh).transpose(2, 0, 1, 3, 4)
    Fb = F.reshape(B, H, nb, Q_BLOCK).transpose(2, 0, 1, 3)
    kpos = jnp.arange(S)

    def block(args):
        q_i, F_i, i = args
        qpos = i * Q_BLOCK + jnp.arange(Q_BLOCK)
        s = jnp.einsum('bhqd,bhkd->bhqk', q_i, k).astype(jnp.float32) * scale
        s = s + (F_i[..., :, None] - F[..., None, :])
        s = jnp.where(kpos[None, :] <= qpos[:, None], s, -jnp.inf)
        p = jax.nn.softmax(s, axis=-1)
        return jnp.einsum('bhqk,bhkd->bhqd', p.astype(v.dtype), v)

    o = lax.map(block, (qb, Fb, jnp.arange(nb)))
    return o.transpose(1, 2, 0, 3, 4).reshape(B, H, S, dh)


def dilated_branch(q, k, v, rel_table, window, dilation):
    B, H, S, dh = q.shape
    d = dilation
    blk = window // d
    L = S // d
    nb = -(-L // blk)
    Lp = nb * blk
    scale = dh ** -0.5

    def regroup(a):
        return a.reshape(B, H, L, d, dh).transpose(0, 1, 3, 2, 4)

    qs = jnp.pad(regroup(q), ((0, 0), (0, 0), (0, 0), (0, Lp - L), (0, 0))).reshape(B, H, d, nb, blk, dh)

    def windows(a):
        ap = jnp.pad(regroup(a), ((0, 0), (0, 0), (0, 0), (blk, Lp - L), (0, 0))).reshape(B, H, d, nb + 1, blk, dh)
        return jnp.concatenate([ap[:, :, :, :-1], ap[:, :, :, 1:]], axis=-2)

    kw, vw = windows(k), windows(v)
    qi = jnp.arange(blk)[:, None]
    ki = jnp.arange(2 * blk)[None, :]
    dist = blk + qi - ki
    key_idx = (jnp.arange(nb) * blk - blk)[:, None, None] + ki[None]
    valid = (dist >= 0) & (dist <= blk) & (key_idx >= 0)
    bias = rel_table[t5_bucket(dist * d)].transpose(2, 0, 1).astype(jnp.float32)
    s = jnp.einsum('bhrnqe,bhrnke->bhrnqk', qs, kw).astype(jnp.float32) * scale
    s = s + bias[None, :, None, None]
    s = jnp.where(valid, s, -jnp.inf)
    m = jnp.max(s, axis=-1, keepdims=True)
    e = jnp.exp(s - m)
    den = jnp.sum(e, axis=-1)
    o = jnp.einsum('bhrnqk,bhrnke->bhrnqe', (e / den[..., None]).astype(v.dtype), vw)
    lse = m[..., 0] + jnp.log(den)
    o = o.reshape(B, H, d, Lp, dh)[:, :, :, :L].transpose(0, 1, 3, 2, 4).reshape(B, H, S, dh)
    lse = lse.reshape(B, H, d, Lp)[..., :L].transpose(0, 1, 3, 2).reshape(B, H, S)
    return o, lse


def dilated_attention(q, k, v, rel_table):
    outs, lses = [], []
    for window, dilation in DIL_PATTERNS:
        o, lse = dilated_branch(q, k, v, rel_table, window, dilation)
        outs.append(o)
        lses.append(lse)
    w = jax.nn.softmax(jnp.stack(lses), axis=0)
    return jnp.einsum('pbhs,pbhsd->bhsd', w.astype(q.dtype), jnp.stack(outs))


def diff_attention(q, k, v, lam, rel_table):
    B, H, _, S, dqk = q.shape
    dv = v.shape[-1]
    nb = S // Q_BLOCK
    scale = dqk ** -0.5
    qb = q.reshape(B, H, 2, nb, Q_BLOCK, dqk).transpose(3, 0, 1, 2, 4, 5)
    kpos = jnp.arange(S)

    def block(args):
        q_i, i = args
        qpos = i * Q_BLOCK + jnp.arange(Q_BLOCK)
        dist = qpos[:, None] - kpos[None, :]
        bias = rel_table[t5_bucket(dist)].transpose(2, 0, 1).astype(jnp.float32)
        s = jnp.einsum('bhmqd,bhmkd->bhmqk', q_i, k).astype(jnp.float32) * scale
        s = s + bias[None, :, None]
        s = jnp.where(dist >= 0, s, -jnp.inf)
        p = jax.nn.softmax(s, axis=-1)
        a = p[:, :, 0] - lam * p[:, :, 1]
        return jnp.einsum('bhqk,bhkd->bhqd', a.astype(v.dtype), v)

    o = lax.map(block, (qb, jnp.arange(nb)))
    return o.transpose(1, 2, 0, 3, 4).reshape(B, H, S, dv)


def hybrid_mixer(h, w_in, b_forget, lam_q1, lam_k1, lam_q2, lam_k2, subln_g, w_out, rel_table, layer_idx):
    B, S, _ = h.shape
    proj = h @ w_in
    fox_qkv, f_logit, dil_qkv, diff_qk, diff_v = jnp.split(proj, SPLIT_IDX, axis=-1)

    qa, ka, va = [to_heads(t, H_FOX) for t in jnp.split(fox_qkv, 3, axis=-1)]
    log_f = jax.nn.log_sigmoid((f_logit + b_forget).astype(jnp.float32)).transpose(0, 2, 1)
    o_a = fox_attention(qa, ka, va, log_f)

    qb, kb, vb = [to_heads(t, H_DIL) for t in jnp.split(dil_qkv, 3, axis=-1)]
    o_b = dilated_attention(qb, kb, vb, rel_table[:, :H_DIL])

    qc, kc = [t.reshape(B, S, H_DIFF, 2, DIFF_QK_DIM).transpose(0, 2, 3, 1, 4)
              for t in jnp.split(diff_qk, 2, axis=-1)]
    vc = to_heads(diff_v, H_DIFF)
    lam_init = 0.8 - 0.6 * math.exp(-0.3 * layer_idx)
    lam = (jnp.exp(jnp.sum(lam_q1.astype(jnp.float32) * lam_k1.astype(jnp.float32)))
           - jnp.exp(jnp.sum(lam_q2.astype(jnp.float32) * lam_k2.astype(jnp.float32))) + lam_init)
    o_c = diff_attention(qc, kc, vc, lam, rel_table[:, H_DIL:])
    o_c = rmsnorm(o_c, subln_g) * (1.0 - lam_init)

    y = jnp.concatenate([from_heads(o_a), from_heads(o_b), from_heads(o_c)], axis=-1)
    return y @ w_out


def squared_relu_mlp(h, w1, w2):
    return jnp.square(jax.nn.relu(h @ w1)) @ w2


def setup_inputs(seed: int = 0) -> dict:
    key = jax.random.key(seed)
    ks = jax.random.split(key, 18)
    f32 = jnp.float32
    nrm = lambda k, shape, s: jax.random.normal(k, shape, f32) * s
    return {
        "x": nrm(ks[0], (BATCH, SEQ, D_MODEL), 1.0),
        "c": nrm(ks[1], (BATCH, D_MODEL), 1.0),
        "w_ada": nrm(ks[2], (DEPTH, D_MODEL, N_MOD * D_MODEL), 0.5 * D_MODEL ** -0.5),
        "b_ada": nrm(ks[3], (DEPTH, N_MOD * D_MODEL), 0.02),
        "norm1_g": 1.0 + nrm(ks[4], (DEPTH, D_MODEL), 0.05),
        "w_in": nrm(ks[5], (DEPTH, D_MODEL, P_IN), D_MODEL ** -0.5),
        "b_forget": 2.0 + nrm(ks[6], (DEPTH, H_FOX), 0.5),
        "lambda_q1": nrm(ks[7], (DEPTH, DIFF_QK_DIM), 0.1),
        "lambda_k1": nrm(ks[8], (DEPTH, DIFF_QK_DIM), 0.1),
        "lambda_q2": nrm(ks[9], (DEPTH, DIFF_QK_DIM), 0.1),
        "lambda_k2": nrm(ks[10], (DEPTH, DIFF_QK_DIM), 0.1),
        "diff_subln_g": 1.0 + nrm(ks[11], (DEPTH, DIFF_V_DIM), 0.05),
        "w_out": nrm(ks[12], (DEPTH, D_MODEL, D_MODEL), D_MODEL ** -0.5),
        "norm2_g": 1.0 + nrm(ks[13], (DEPTH, D_MODEL), 0.05),
        "w_mlp1": nrm(ks[14], (DEPTH, D_MODEL, D_FF), D_MODEL ** -0.5),
        "w_mlp2": nrm(ks[15], (DEPTH, D_FF, D_MODEL), D_FF ** -0.5),
        "rel_table": nrm(ks[16], (NUM_BUCKETS, H_DIL + H_DIFF), 0.5),
        "final_norm_g": 1.0 + nrm(ks[17], (D_MODEL,), 0.05),
    }


def reference(x, c, w_ada, b_ada, norm1_g, w_in, b_forget, lambda_q1, lambda_k1, lambda_q2, lambda_k2,
              diff_subln_g, w_out, norm2_g, w_mlp1, w_mlp2, rel_table, final_norm_g):
    for l in range(DEPTH):
        mod = c @ w_ada[l] + b_ada[l]
        sh1, sc1, g1, sh2, sc2, g2 = [m[:, None, :] for m in jnp.split(mod, N_MOD, axis=-1)]
        h = rmsnorm(x, norm1_g[l]) * (1.0 + sc1) + sh1
        x = x + g1 * hybrid_mixer(h, w_in[l], b_forget[l], lambda_q1[l], lambda_k1[l], lambda_q2[l],
                                  lambda_k2[l], diff_subln_g[l], w_out[l], rel_table, l)
        h = rmsnorm(x, norm2_g[l]) * (1.0 + sc2) + sh2
        x = x + g2 * squared_relu_mlp(h, w_mlp1[l], w_mlp2[l])
    return rmsnorm(x, final_norm_g)
```

```python
import functools
import math

import numpy as np
import jax
import jax.numpy as jnp
from jax import lax
from jax.experimental import pallas as pl
from jax.experimental.pallas import tpu as pltpu

F32 = jnp.float32
BF16 = jnp.bfloat16

D_MODEL = 2048
HEAD_DIM = 128
N_HEADS = D_MODEL // HEAD_DIM
H_DIFF = N_HEADS // 4
H_FOX = (N_HEADS - H_DIFF) // 2
H_DIL = N_HEADS - H_FOX - H_DIFF
DIFF_QK_DIM = HEAD_DIM // 2
DIL_PATTERNS = ((128, 1), (512, 4), (2048, 16))
DIL_BLK = 128
NUM_BUCKETS = 32
MAX_DISTANCE = 2048
D_FF = 4 * D_MODEL
N_MOD = 6
RMS_EPS = 1e-6
FOX_W = H_FOX * HEAD_DIM
DIL_W = H_DIL * HEAD_DIM
DIFF_W = H_DIFF * HEAD_DIM
NEG = -1e30

LANES = 128
SUBLANES = 8
VMEM_LIMIT_BYTES = 56 * 1024 * 1024


def _t5_thresholds():
    max_exact = NUM_BUCKETS // 2
    n = np.arange(max_exact, 2 * MAX_DISTANCE, dtype=np.float64)
    val = np.log(n / max_exact) / math.log(MAX_DISTANCE / max_exact) * (NUM_BUCKETS - max_exact)
    inside = n < MAX_DISTANCE
    frac = np.abs(val - np.round(val))
    assert np.all((frac[inside] > 1e-5) | (frac[inside] == 0.0))
    large = np.minimum(max_exact + np.floor(val).astype(np.int64), NUM_BUCKETS - 1)
    thr = list(range(max_exact))
    for b in range(max_exact, NUM_BUCKETS):
        thr.append(int(n[np.nonzero(large >= b)[0][0]]))
    return tuple(thr)


T5_THRESHOLDS = _t5_thresholds()


def _cparams(n_grid):
    return pltpu.CompilerParams(dimension_semantics=("arbitrary",) * n_grid,
                                vmem_limit_bytes=VMEM_LIMIT_BYTES)


def _dot(a, b):
    return jnp.dot(a, b, preferred_element_type=F32)


def _dot_nt(a, b):
    return lax.dot_general(a, b, (((1,), (1,)), ((), ())), preferred_element_type=F32)


def _ada_kernel(c_ref, w_ref, b_ref, o_ref):
    o_ref[...] = _dot(c_ref[...].astype(BF16), w_ref[...].astype(BF16)) + b_ref[...]


def _ada_mod(c, w_ada, b_ada, tn=512):
    depth, d, n = w_ada.shape
    b = c.shape[0]
    return pl.pallas_call(
        _ada_kernel,
        grid=(depth, n // tn),
        in_specs=[
            pl.BlockSpec((b, d), lambda l, j: (0, 0)),
            pl.BlockSpec((None, d, tn), lambda l, j: (l, 0, j)),
            pl.BlockSpec((None, 1, tn), lambda l, j: (l, 0, j)),
        ],
        out_specs=pl.BlockSpec((None, b, tn), lambda l, j: (l, 0, j)),
        out_shape=jax.ShapeDtypeStruct((depth, b, n), F32),
        compiler_params=_cparams(2),
        name="ada_mod",
    )(c, w_ada, b_ada.reshape(depth, 1, n))


def _t5_bias_of(n, table_ref, col):
    val = jnp.full(n.shape, table_ref[0, col], F32)
    for b in range(1, NUM_BUCKETS):
        val = jnp.where(n >= T5_THRESHOLDS[b], table_ref[b, col], val)
    return val


def _diff_bias_kernel(table_ref, o_ref, *, n_tiles):
    h = pl.program_id(0)
    row = lax.broadcasted_iota(jnp.int32, (LANES, LANES), 0)
    col = lax.broadcasted_iota(jnp.int32, (LANES, LANES), 1)

    def body(t, carry):
        o_ref[t] = _t5_bias_of(t * LANES + row - col, table_ref, H_DIL + h)
        return carry

    lax.fori_loop(0, n_tiles, body, 0)


def _diff_bias_tiles(rel_table, seq):
    n_tiles = seq // LANES
    return pl.pallas_call(
        functools.partial(_diff_bias_kernel, n_tiles=n_tiles),
        grid=(H_DIFF,),
        in_specs=[pl.BlockSpec(memory_space=pltpu.SMEM)],
        out_specs=pl.BlockSpec((None, n_tiles, LANES, LANES), lambda h: (h, 0, 0, 0)),
        out_shape=jax.ShapeDtypeStruct((H_DIFF, n_tiles, LANES, LANES), F32),
        compiler_params=_cparams(1),
        name="diff_bias_tiles",
    )(rel_table)


def _dil_bias_kernel(table_ref, o_ref):
    h = pl.program_id(0)
    qi = lax.broadcasted_iota(jnp.int32, (DIL_BLK, 2 * DIL_BLK), 0)
    ki = lax.broadcasted_iota(jnp.int32, (DIL_BLK, 2 * DIL_BLK), 1)
    dist = DIL_BLK + qi - ki
    valid = (dist >= 0) & (dist <= DIL_BLK)
    for p, (_, dil) in enumerate(DIL_PATTERNS):
        o_ref[p] = jnp.where(valid, _t5_bias_of(dist * dil, table_ref, h), NEG)


def _dil_bias_tiles(rel_table):
    n_pat = len(DIL_PATTERNS)
    return pl.pallas_call(
        _dil_bias_kernel,
        grid=(H_DIL,),
        in_specs=[pl.BlockSpec(memory_space=pltpu.SMEM)],
        out_specs=pl.BlockSpec((None, n_pat, DIL_BLK, 2 * DIL_BLK), lambda h: (h, 0, 0, 0)),
        out_shape=jax.ShapeDtypeStruct((H_DIL, n_pat, DIL_BLK, 2 * DIL_BLK), F32),
        compiler_params=_cparams(1),
        name="dil_bias_tiles",
    )(rel_table)


def _norm_matmul_kernel(x_ref, g_ref, sh_ref, sc_ref, w_ref, cs_ref, o_ref, h_scr, *, relu2):
    @pl.when(pl.program_id(1) == 0)
    def _():
        x = x_ref[...]
        y = x * lax.rsqrt(jnp.mean(x * x, axis=-1, keepdims=True) + RMS_EPS)
        y = y * g_ref[...]
        h_scr[...] = (y * (1.0 + sc_ref[...]) + sh_ref[...]).astype(BF16)

    acc = _dot(h_scr[...], w_ref[...])
    if relu2:
        acc = jnp.square(jnp.maximum(acc, 0.0))
    o_ref[...] = (acc * cs_ref[...]).astype(o_ref.dtype)


def _norm_matmul(x, g, mod3, shift_idx, scale_idx, w, col_scale, out_dtype, *, seq, tm, tn, relu2=False):
    m, d = x.shape
    n = w.shape[1]
    tiles_per_seq = seq // tm
    return pl.pallas_call(
        functools.partial(_norm_matmul_kernel, relu2=relu2),
        grid=(m // tm, n // tn),
        in_specs=[
            pl.BlockSpec((tm, d), lambda i, j: (i, 0)),
            pl.BlockSpec((1, d), lambda i, j: (0, 0)),
            pl.BlockSpec((None, 1, d), lambda i, j: (i // tiles_per_seq, 0, shift_idx)),
            pl.BlockSpec((None, 1, d), lambda i, j: (i // tiles_per_seq, 0, scale_idx)),
            pl.BlockSpec((d, tn), lambda i, j: (0, j)),
            pl.BlockSpec((1, tn), lambda i, j: (0, j)),
        ],
        out_specs=pl.BlockSpec((tm, tn), lambda i, j: (i, j)),
        out_shape=jax.ShapeDtypeStruct((m, n), out_dtype),
        scratch_shapes=[pltpu.VMEM((tm, d), BF16)],
        compiler_params=_cparams(2),
        name="norm_matmul_relu2" if relu2 else "norm_matmul",
    )(x, g, mod3, mod3, w, col_scale)


def _gated_matmul_kernel(*refs, n_lhs):
    lhs = refs[:n_lhs]
    ws = refs[n_lhs:2 * n_lhs]
    x_ref, gate_ref, o_ref = refs[2 * n_lhs:]
    acc = _dot(lhs[0][...], ws[0][...])
    for a_ref, w_ref in zip(lhs[1:], ws[1:]):
        acc = acc + _dot(a_ref[...], w_ref[...])
    o_ref[...] = x_ref[...] + gate_ref[...] * acc


def _gated_matmul(lhs_list, w, x, mod3, gate_idx, *, seq, tm, tn):
    m, n = x.shape
    n_lhs = len(lhs_list)
    tiles_per_seq = seq // tm
    tn_per_d = n // tn
    in_specs, row0 = [], 0
    for a in lhs_list:
        in_specs.append(pl.BlockSpec((tm, a.shape[1]), lambda i, j: (i, 0)))
    for a in lhs_list:
        kp = a.shape[1]
        assert row0 % kp == 0
        in_specs.append(pl.BlockSpec((kp, tn), functools.partial(lambda i, j, r: (r, j), r=row0 // kp)))
        row0 += kp
    assert row0 == w.shape[0]
    in_specs.append(pl.BlockSpec((tm, tn), lambda i, j: (i, j)))
    in_specs.append(pl.BlockSpec((None, 1, tn), lambda i, j: (i // tiles_per_seq, 0, gate_idx * tn_per_d + j)))
    return pl.pallas_call(
        functools.partial(_gated_matmul_kernel, n_lhs=n_lhs),
        grid=(m // tm, n // tn),
        in_specs=in_specs,
        out_specs=pl.BlockSpec((tm, tn), lambda i, j: (i, j)),
        out_shape=jax.ShapeDtypeStruct((m, n), F32),
        compiler_params=_cparams(2),
        name="gated_matmul",
    )(*lhs_list, *([w] * n_lhs), x, mod3)


def _split3_bf16(x):
    hi = x.astype(BF16)
    r1 = x - hi.astype(F32)
    mid = r1.astype(BF16)
    lo = (r1 - mid.astype(F32)).astype(BF16)
    return hi, mid, lo


def _decay_cumsum_kernel(f_ref, b_ref, o_ref, *, n_blocks):
    row = lax.broadcasted_iota(jnp.int32, (LANES, LANES), 0)
    col = lax.broadcasted_iota(jnp.int32, (LANES, LANES), 1)
    tri = (col <= row).astype(BF16)

    def body(i, carry):
        rows = pl.ds(pl.multiple_of(i * LANES, LANES), LANES)
        z = f_ref[rows, :] + b_ref[...]
        log_f = jnp.minimum(z, 0.0) - jnp.log1p(jnp.exp(-jnp.abs(z)))
        hi, mid, lo = _split3_bf16(log_f)
        pre = (_dot(tri, hi) + _dot(tri, mid)) + _dot(tri, lo) + carry
        o_ref[rows, :] = pre
        return pre[LANES - 1:LANES, :]

    lax.fori_loop(0, n_blocks, body, jnp.zeros((1, LANES), F32))


def _decay_cumsum(proj_b, f_col_block, b_forget_pad, *, batch, seq):
    return pl.pallas_call(
        functools.partial(_decay_cumsum_kernel, n_blocks=seq // LANES),
        grid=(batch,),
        in_specs=[
            pl.BlockSpec((seq, LANES), lambda b: (b, f_col_block)),
            pl.BlockSpec((1, LANES), lambda b: (0, 0)),
        ],
        out_specs=pl.BlockSpec((None, seq, LANES), lambda b: (b, 0, 0)),
        out_shape=jax.ShapeDtypeStruct((batch, seq, LANES), F32),
        compiler_params=_cparams(1),
        name="decay_cumsum",
    )(proj_b, b_forget_pad)


def _online_softmax_step(s, v, m_scr, l_scr, acc_scr):
    m_prev = m_scr[...]
    m_new = jnp.maximum(m_prev, jnp.max(s, axis=-1, keepdims=True))
    alpha = jnp.exp(m_prev - m_new)
    p = jnp.exp(s - m_new)
    l_scr[...] = alpha * l_scr[...] + jnp.sum(p, axis=-1, keepdims=True)
    acc_scr[...] = alpha * acc_scr[...] + _dot(p.astype(BF16), v)
    m_scr[...] = m_new


def _fox_kernel(q_ref, k_ref, v_ref, fcol_ref, frow_ref, o_ref, m_scr, l_scr, acc_scr, *, tq, tk):
    qi = pl.program_id(2)
    m_scr[...] = jnp.full(m_scr.shape, NEG, F32)
    l_scr[...] = jnp.zeros(l_scr.shape, F32)
    acc_scr[...] = jnp.zeros(acc_scr.shape, F32)
    q = q_ref[...]
    f_i = fcol_ref[...]

    def logits(kb):
        cols = pl.ds(pl.multiple_of(kb * tk, tk), tk)
        s = _dot_nt(q, k_ref[cols, :])
        return s + (f_i - frow_ref[pl.ds(kb, 1), :]), v_ref[cols, :]

    def body(kb, carry):
        s, v = logits(kb)
        _online_softmax_step(s, v, m_scr, l_scr, acc_scr)
        return carry

    n_full = (qi * tq) // tk
    lax.fori_loop(0, n_full, body, 0)
    s, v = logits(n_full)
    row = lax.broadcasted_iota(jnp.int32, (tq, tk), 0) + (qi * tq - n_full * tk)
    col = lax.broadcasted_iota(jnp.int32, (tq, tk), 1)
    _online_softmax_step(jnp.where(col <= row, s, NEG), v, m_scr, l_scr, acc_scr)
    o_ref[...] = (acc_scr[...] / l_scr[...]).astype(o_ref.dtype)


def _fox_attention(proj, q_blk0, k_blk0, v_blk0, f_col, f_row, *, batch, seq, tq=512, tk=512):
    nq = seq // tq
    return pl.pallas_call(
        functools.partial(_fox_kernel, tq=tq, tk=tk),
        grid=(batch, H_FOX, nq),
        in_specs=[
            pl.BlockSpec((tq, HEAD_DIM), lambda b, h, i: (b * nq + i, q_blk0 + h)),
            pl.BlockSpec((seq, HEAD_DIM), lambda b, h, i: (b, k_blk0 + h)),
            pl.BlockSpec((seq, HEAD_DIM), lambda b, h, i: (b, v_blk0 + h)),
            pl.BlockSpec((None, None, tq, 1), lambda b, h, i: (b, h, i, 0)),
            pl.BlockSpec((None, None, seq // tk, tk), lambda b, h, i: (b, h, 0, 0)),
        ],
        out_specs=pl.BlockSpec((tq, HEAD_DIM), lambda b, h, i: (b * nq + i, h)),
        out_shape=jax.ShapeDtypeStruct((batch * seq, FOX_W), BF16),
        scratch_shapes=[pltpu.VMEM((tq, 1), F32), pltpu.VMEM((tq, 1), F32), pltpu.VMEM((tq, HEAD_DIM), F32)],
        compiler_params=_cparams(3),
        name="fox_attention",
    )(proj, proj, proj, f_col, f_row)


def _diff_kernel(q_ref, k_ref, v_ref, bias_ref, lam_ref, g_ref, o_ref, m_scr, l_scr, acc_scr,
                 *, tq, tk, lam_init):
    qi = pl.program_id(2)
    m_scr[...] = jnp.full(m_scr.shape, NEG, F32)
    l_scr[...] = jnp.zeros(l_scr.shape, F32)
    acc_scr[...] = jnp.zeros(acc_scr.shape, F32)
    q = q_ref[...]
    lane = lax.broadcasted_iota(jnp.int32, q.shape, 1)
    zero = jnp.zeros_like(q)
    q2 = jnp.concatenate([jnp.where(lane < DIFF_QK_DIM, q, zero), jnp.where(lane >= DIFF_QK_DIM, q, zero)], axis=0)
    r_sub, c_sub = tq // LANES, tk // LANES

    def logits(kb):
        cols = pl.ds(pl.multiple_of(kb * tk, tk), tk)
        s = _dot_nt(q2, k_ref[cols, :])
        t0 = qi * r_sub - kb * c_sub
        rows = []
        for a in range(r_sub):
            rows.append(jnp.concatenate(
                [bias_ref[jnp.maximum(t0 + a - c, 0)] for c in range(c_sub)], axis=1))
        bias = jnp.concatenate(rows, axis=0)
        return s + jnp.concatenate([bias, bias], axis=0), v_ref[cols, :]

    def body(kb, carry):
        s, v = logits(kb)
        _online_softmax_step(s, v, m_scr, l_scr, acc_scr)
        return carry

    n_full = (qi * tq) // tk
    lax.fori_loop(0, n_full, body, 0)
    s, v = logits(n_full)
    row = lax.broadcasted_iota(jnp.int32, (2 * tq, tk), 0)
    row = jnp.where(row >= tq, row - tq, row) + (qi * tq - n_full * tk)
    col = lax.broadcasted_iota(jnp.int32, (2 * tq, tk), 1)
    _online_softmax_step(jnp.where(col <= row, s, NEG), v, m_scr, l_scr, acc_scr)

    lam_v = lam_ref[...]
    lam = (jnp.exp(jnp.sum(lam_v[0:1] * lam_v[1:2], axis=-1, keepdims=True))
           - jnp.exp(jnp.sum(lam_v[2:3] * lam_v[3:4], axis=-1, keepdims=True)) + lam_init)
    o_all = acc_scr[...] / l_scr[...]
    o = o_all[:tq] - lam * o_all[tq:]
    y = o * lax.rsqrt(jnp.mean(o * o, axis=-1, keepdims=True) + RMS_EPS)
    o_ref[...] = ((y * g_ref[...]) * (1.0 - lam_init)).astype(o_ref.dtype)


def _diff_attention(proj, q_blk0, k_blk0, v_blk0, bias_tiles, lam_rows, subln_g, *, batch, seq, lam_init,
                    tq=256, tk=512):
    nq = seq // tq
    n_tiles = seq // LANES
    return pl.pallas_call(
        functools.partial(_diff_kernel, tq=tq, tk=tk, lam_init=lam_init),
        grid=(batch, H_DIFF, nq),
        in_specs=[
            pl.BlockSpec((tq, HEAD_DIM), lambda b, h, i: (b * nq + i, q_blk0 + h)),
            pl.BlockSpec((seq, HEAD_DIM), lambda b, h, i: (b, k_blk0 + h)),
            pl.BlockSpec((seq, HEAD_DIM), lambda b, h, i: (b, v_blk0 + h)),
            pl.BlockSpec((None, n_tiles, LANES, LANES), lambda b, h, i: (h, 0, 0, 0)),
            pl.BlockSpec((4, DIFF_QK_DIM), lambda b, h, i: (0, 0)),
            pl.BlockSpec((1, HEAD_DIM), lambda b, h, i: (0, 0)),
        ],
        out_specs=pl.BlockSpec((tq, HEAD_DIM), lambda b, h, i: (b * nq + i, h)),
        out_shape=jax.ShapeDtypeStruct((batch * seq, DIFF_W), BF16),
        scratch_shapes=[pltpu.VMEM((2 * tq, 1), F32), pltpu.VMEM((2 * tq, 1), F32),
                        pltpu.VMEM((2 * tq, HEAD_DIM), F32)],
        compiler_params=_cparams(3),
        name="diff_attention",
    )(proj, proj, proj, bias_tiles, lam_rows, subln_g)


def _dil_kernel(q_ref, k_ref, v_ref, bias_ref, o_ref, o_scr, lse_scr, *, seq):
    blk = DIL_BLK
    for p, (_, dil) in enumerate(DIL_PATTERNS):
        n_blocks = seq // (dil * blk)

        def body(it, carry, p=p, dil=dil, n_blocks=n_blocks):
            r = it // n_blocks
            n = it - r * n_blocks
            cur = pl.ds(r + n * (blk * dil), blk, stride=dil)
            prev = pl.ds(r + jnp.maximum(n - 1, 0) * (blk * dil), blk, stride=dil)
            q = q_ref[cur, :].astype(BF16)
            bias = bias_ref[p]
            s_prev = _dot_nt(q, k_ref[prev, :].astype(BF16)) + bias[:, :blk]
            s_prev = jnp.where(n > 0, s_prev, NEG)
            s_cur = _dot_nt(q, k_ref[cur, :].astype(BF16)) + bias[:, blk:]
            m = jnp.maximum(jnp.max(s_prev, axis=-1, keepdims=True), jnp.max(s_cur, axis=-1, keepdims=True))
            e_prev = jnp.exp(s_prev - m)
            e_cur = jnp.exp(s_cur - m)
            den = jnp.sum(e_prev, axis=-1, keepdims=True) + jnp.sum(e_cur, axis=-1, keepdims=True)
            inv = 1.0 / den
            o = (_dot((e_prev * inv).astype(BF16), v_ref[prev, :].astype(BF16))
                 + _dot((e_cur * inv).astype(BF16), v_ref[cur, :].astype(BF16)))
            o_scr[p, cur, :] = o
            lse_scr[p, cur, :] = jnp.broadcast_to(m + jnp.log(den), (blk, LANES))
            return carry

        lax.fori_loop(0, dil * n_blocks, body, 0)

    chunk = 512

    def mix(i, carry):
        rows = pl.ds(pl.multiple_of(i * chunk, chunk), chunk)
        lse = [lse_scr[p, rows, :] for p in range(len(DIL_PATTERNS))]
        mx = functools.reduce(jnp.maximum, lse)
        w = [jnp.exp(x - mx) for x in lse]
        inv = 1.0 / functools.reduce(lambda a, b: a + b, w)
        acc = (w[0] * inv) * o_scr[0, rows, :]
        for p in range(1, len(DIL_PATTERNS)):
            acc = acc + (w[p] * inv) * o_scr[p, rows, :]
        o_ref[rows, :] = acc.astype(o_ref.dtype)
        return carry

    lax.fori_loop(0, seq // chunk, mix, 0)


def _dil_attention(proj, q_blk0, k_blk0, v_blk0, bias_tiles, *, batch, seq):
    n_pat = len(DIL_PATTERNS)
    return pl.pallas_call(
        functools.partial(_dil_kernel, seq=seq),
        grid=(batch, H_DIL),
        in_specs=[
            pl.BlockSpec((seq, HEAD_DIM), lambda b, h: (b, q_blk0 + h)),
            pl.BlockSpec((seq, HEAD_DIM), lambda b, h: (b, k_blk0 + h)),
            pl.BlockSpec((seq, HEAD_DIM), lambda b, h: (b, v_blk0 + h)),
            pl.BlockSpec((None, n_pat, DIL_BLK, 2 * DIL_BLK), lambda b, h: (h, 0, 0, 0)),
        ],
        out_specs=pl.BlockSpec((seq, HEAD_DIM), lambda b, h: (b, h)),
        out_shape=jax.ShapeDtypeStruct((batch * seq, DIL_W), BF16),
        scratch_shapes=[pltpu.VMEM((n_pat, seq, HEAD_DIM), F32), pltpu.VMEM((n_pat, seq, LANES), F32)],
        compiler_params=_cparams(2),
        name="dil_attention",
    )(proj, proj, proj, bias_tiles)


def _rmsnorm_kernel(x_ref, g_ref, o_ref):
    x = x_ref[...]
    o_ref[...] = (x * lax.rsqrt(jnp.mean(x * x, axis=-1, keepdims=True) + RMS_EPS)) * g_ref[...]


def _rmsnorm(x, g, tm=512):
    m, d = x.shape
    return pl.pallas_call(
        _rmsnorm_kernel,
        grid=(m // tm,),
        in_specs=[pl.BlockSpec((tm, d), lambda i: (i, 0)), pl.BlockSpec((1, d), lambda i: (0, 0))],
        out_specs=pl.BlockSpec((tm, d), lambda i: (i, 0)),
        out_shape=jax.ShapeDtypeStruct((m, d), F32),
        compiler_params=_cparams(1),
        name="final_rmsnorm",
    )(x, g)


def _split_w_in(w_in_l):
    fox_qkv = 3 * FOX_W
    dil_qkv = 3 * DIL_W
    o_f = fox_qkv
    o_dil = o_f + H_FOX
    o_dqk = o_dil + dil_qkv
    w_a = jnp.concatenate([w_in_l[:, :fox_qkv], w_in_l[:, o_dqk:]], axis=1).astype(BF16)
    pad = jnp.zeros((w_in_l.shape[0], 2 * LANES - H_FOX), w_in_l.dtype)
    w_b = jnp.concatenate([w_in_l[:, o_dil:o_dqk], w_in_l[:, o_f:o_dil], pad], axis=1).astype(BF16)
    return w_a, w_b


def kernel(x, c, w_ada, b_ada, norm1_g, w_in, b_forget, lambda_q1, lambda_k1, lambda_q2, lambda_k2,
           diff_subln_g, w_out, norm2_g, w_mlp1, w_mlp2, rel_table, final_norm_g):
    batch, seq, d = x.shape
    depth = w_ada.shape[0]
    assert d == D_MODEL and seq % (DIL_PATTERNS[-1][1] * DIL_BLK) == 0

    c_rows = jnp.zeros((SUBLANES, d), F32).at[:batch].set(c)
    mod = _ada_mod(c_rows, w_ada, b_ada)[:, :batch]
    diff_bias = _diff_bias_tiles(rel_table, seq)
    dil_bias = _dil_bias_tiles(rel_table)

    scale_a = jnp.ones((1, 3 * FOX_W + 3 * DIFF_W), F32)
    scale_a = scale_a.at[:, :FOX_W].set(HEAD_DIM ** -0.5)
    scale_a = scale_a.at[:, 3 * FOX_W:3 * FOX_W + DIFF_W].set(DIFF_QK_DIM ** -0.5)
    scale_b = jnp.ones((1, 3 * DIL_W + 2 * LANES), F32).at[:, :DIL_W].set(HEAD_DIM ** -0.5)
    ones_ff = jnp.ones((1, D_FF), F32)

    xf = x.reshape(batch * seq, d)
    for l in range(depth):
        mod3 = mod[l].reshape(batch, 1, N_MOD * d)
        w_a, w_b = _split_w_in(w_in[l])
        proj_a = _norm_matmul(xf, norm1_g[l].reshape(1, d), mod3, 0, 1, w_a, scale_a, BF16,
                              seq=seq, tm=512, tn=768)
        proj_b = _norm_matmul(xf, norm1_g[l].reshape(1, d), mod3, 0, 1, w_b, scale_b, F32,
                              seq=seq, tm=512, tn=512)

        b_f = jnp.zeros((1, LANES), F32).at[0, :H_FOX].set(b_forget[l])
        f_cum = _decay_cumsum(proj_b, 3 * DIL_W // LANES, b_f, batch=batch, seq=seq)
        f_heads = jnp.transpose(f_cum[:, :, :H_FOX], (0, 2, 1))
        y_a = _fox_attention(proj_a, 0, H_FOX, 2 * H_FOX,
                             f_heads.reshape(batch, H_FOX, seq, 1), f_heads.reshape(batch, H_FOX, seq // 512, 512),
                             batch=batch, seq=seq, tq=512, tk=512)

        y_b = _dil_attention(proj_b, 0, H_DIL, 2 * H_DIL, dil_bias, batch=batch, seq=seq)

        lam_init = 0.8 - 0.6 * math.exp(-0.3 * l)
        lam_rows = jnp.stack([lambda_q1[l], lambda_k1[l], lambda_q2[l], lambda_k2[l]]).astype(F32)
        c0 = 3 * H_FOX
        y_c = _diff_attention(proj_a, c0, c0 + H_DIFF, c0 + 2 * H_DIFF, diff_bias, lam_rows,
                              diff_subln_g[l].reshape(1, HEAD_DIM), batch=batch, seq=seq, lam_init=lam_init)

        xf = _gated_matmul([y_a, y_b, y_c], w_out[l].astype(BF16), xf, mod3, 2, seq=seq, tm=512, tn=1024)

        hidden = _norm_matmul(xf, norm2_g[l].reshape(1, d), mod3, 3, 4, w_mlp1[l].astype(BF16), ones_ff, BF16,
                              seq=seq, tm=512, tn=1024, relu2=True)
        xf = _gated_matmul([hidden], w_mlp2[l].astype(BF16), xf, mod3, 5, seq=seq, tm=512, tn=512)

    out = _rmsnorm(xf, final_norm_g.reshape(1, d))
    return out.reshape(batch, seq, d)
```

```python
import functools
import math

import numpy as np
import jax
import jax.numpy as jnp
from jax import lax
from jax.experimental import pallas as pl
from jax.experimental.pallas import tpu as pltpu

F32 = jnp.float32
BF16 = jnp.bfloat16

D_MODEL = 2048
HEAD_DIM = 128
N_HEADS = D_MODEL // HEAD_DIM
H_DIFF = N_HEADS // 4
H_FOX = (N_HEADS - H_DIFF) // 2
H_DIL = N_HEADS - H_FOX - H_DIFF
DIFF_QK_DIM = HEAD_DIM // 2
DIL_PATTERNS = ((128, 1), (512, 4), (2048, 16))
DIL_BLK = 128
DIL_UNROLL = 4
NUM_BUCKETS = 32
MAX_DISTANCE = 2048
D_FF = 4 * D_MODEL
N_MOD = 6
RMS_EPS = 1e-6
FOX_W = H_FOX * HEAD_DIM
DIL_W = H_DIL * HEAD_DIM
DIFF_W = H_DIFF * HEAD_DIM
NEG = -1e30

LANES = 128
SUBLANES = 8
VMEM_LIMIT_BYTES = 56 * 1024 * 1024


def _t5_thresholds():
    max_exact = NUM_BUCKETS // 2
    n = np.arange(max_exact, 2 * MAX_DISTANCE, dtype=np.float64)
    val = np.log(n / max_exact) / math.log(MAX_DISTANCE / max_exact) * (NUM_BUCKETS - max_exact)
    inside = n < MAX_DISTANCE
    frac = np.abs(val - np.round(val))
    assert np.all((frac[inside] > 1e-5) | (frac[inside] == 0.0))
    large = np.minimum(max_exact + np.floor(val).astype(np.int64), NUM_BUCKETS - 1)
    thr = list(range(max_exact))
    for b in range(max_exact, NUM_BUCKETS):
        thr.append(int(n[np.nonzero(large >= b)[0][0]]))
    return tuple(thr)


T5_THRESHOLDS = _t5_thresholds()


def _cparams(n_grid):
    return pltpu.CompilerParams(dimension_semantics=("arbitrary",) * n_grid,
                                vmem_limit_bytes=VMEM_LIMIT_BYTES)


def _dot(a, b):
    return jnp.dot(a, b, preferred_element_type=F32)


def _dot_nt(a, b):
    return lax.dot_general(a, b, (((1,), (1,)), ((), ())), preferred_element_type=F32)


def _ada_kernel(c_ref, w_ref, b_ref, o_ref):
    o_ref[...] = _dot(c_ref[...].astype(BF16), w_ref[...].astype(BF16)) + b_ref[...]


def _ada_mod(c, w_ada, b_ada, tn=512):
    depth, d, n = w_ada.shape
    b = c.shape[0]
    return pl.pallas_call(
        _ada_kernel,
        grid=(depth, n // tn),
        in_specs=[
            pl.BlockSpec((b, d), lambda l, j: (0, 0)),
            pl.BlockSpec((None, d, tn), lambda l, j: (l, 0, j)),
            pl.BlockSpec((None, 1, tn), lambda l, j: (l, 0, j)),
        ],
        out_specs=pl.BlockSpec((None, b, tn), lambda l, j: (l, 0, j)),
        out_shape=jax.ShapeDtypeStruct((depth, b, n), F32),
        compiler_params=_cparams(2),
        name="ada_mod",
    )(c, w_ada, b_ada.reshape(depth, 1, n))


def _t5_bias_of(n, table_ref, col):
    val = jnp.full(n.shape, table_ref[0, col], F32)
    for b in range(1, NUM_BUCKETS):
        val = jnp.where(n >= T5_THRESHOLDS[b], table_ref[b, col], val)
    return val


def _diff_bias_kernel(table_ref, o_ref, *, n_tiles):
    h = pl.program_id(0)
    row = lax.broadcasted_iota(jnp.int32, (LANES, LANES), 0)
    col = lax.broadcasted_iota(jnp.int32, (LANES, LANES), 1)

    def body(t, carry):
        o_ref[t] = _t5_bias_of(t * LANES + row - col, table_ref, H_DIL + h)
        return carry

    lax.fori_loop(0, n_tiles, body, 0)


def _diff_bias_tiles(rel_table, seq):
    n_tiles = seq // LANES
    return pl.pallas_call(
        functools.partial(_diff_bias_kernel, n_tiles=n_tiles),
        grid=(H_DIFF,),
        in_specs=[pl.BlockSpec(memory_space=pltpu.SMEM)],
        out_specs=pl.BlockSpec((None, n_tiles, LANES, LANES), lambda h: (h, 0, 0, 0)),
        out_shape=jax.ShapeDtypeStruct((H_DIFF, n_tiles, LANES, LANES), F32),
        compiler_params=_cparams(1),
        name="diff_bias_tiles",
    )(rel_table)


def _dil_bias_kernel(table_ref, o_ref):
    h = pl.program_id(0)
    qi = lax.broadcasted_iota(jnp.int32, (DIL_BLK, 2 * DIL_BLK), 0)
    ki = lax.broadcasted_iota(jnp.int32, (DIL_BLK, 2 * DIL_BLK), 1)
    dist = DIL_BLK + qi - ki
    valid = (dist >= 0) & (dist <= DIL_BLK)
    for p, (_, dil) in enumerate(DIL_PATTERNS):
        o_ref[p] = jnp.where(valid, _t5_bias_of(dist * dil, table_ref, h), NEG)


def _dil_bias_tiles(rel_table):
    n_pat = len(DIL_PATTERNS)
    return pl.pallas_call(
        _dil_bias_kernel,
        grid=(H_DIL,),
        in_specs=[pl.BlockSpec(memory_space=pltpu.SMEM)],
        out_specs=pl.BlockSpec((None, n_pat, DIL_BLK, 2 * DIL_BLK), lambda h: (h, 0, 0, 0)),
        out_shape=jax.ShapeDtypeStruct((H_DIL, n_pat, DIL_BLK, 2 * DIL_BLK), F32),
        compiler_params=_cparams(1),
        name="dil_bias_tiles",
    )(rel_table)


def _norm_matmul_kernel(x_ref, g_ref, sh_ref, sc_ref, w_ref, cs_ref, o_ref, h_scr, *, relu2):
    @pl.when(pl.program_id(1) == 0)
    def _():
        x = x_ref[...]
        y = x * lax.rsqrt(jnp.mean(x * x, axis=-1, keepdims=True) + RMS_EPS)
        y = y * g_ref[...]
        h_scr[...] = (y * (1.0 + sc_ref[...]) + sh_ref[...]).astype(BF16)

    acc = _dot(h_scr[...], w_ref[...])
    if relu2:
        acc = jnp.square(jnp.maximum(acc, 0.0))
    o_ref[...] = (acc * cs_ref[...]).astype(o_ref.dtype)


def _norm_matmul(x, g, mod3, shift_idx, scale_idx, w, col_scale, out_dtype, *, seq, tm, tn, relu2=False):
    m, d = x.shape
    n = w.shape[1]
    tiles_per_seq = seq // tm
    return pl.pallas_call(
        functools.partial(_norm_matmul_kernel, relu2=relu2),
        grid=(m // tm, n // tn),
        in_specs=[
            pl.BlockSpec((tm, d), lambda i, j: (i, 0)),
            pl.BlockSpec((1, d), lambda i, j: (0, 0)),
            pl.BlockSpec((None, 1, d), lambda i, j: (i // tiles_per_seq, 0, shift_idx)),
            pl.BlockSpec((None, 1, d), lambda i, j: (i // tiles_per_seq, 0, scale_idx)),
            pl.BlockSpec((d, tn), lambda i, j: (0, j)),
            pl.BlockSpec((1, tn), lambda i, j: (0, j)),
        ],
        out_specs=pl.BlockSpec((tm, tn), lambda i, j: (i, j)),
        out_shape=jax.ShapeDtypeStruct((m, n), out_dtype),
        scratch_shapes=[pltpu.VMEM((tm, d), BF16)],
        compiler_params=_cparams(2),
        name="norm_matmul_relu2" if relu2 else "norm_matmul",
    )(x, g, mod3, mod3, w, col_scale)


def _gated_matmul_kernel(*refs, n_lhs):
    lhs = refs[:n_lhs]
    ws = refs[n_lhs:2 * n_lhs]
    x_ref, gate_ref, o_ref = refs[2 * n_lhs:]
    acc = _dot(lhs[0][...], ws[0][...])
    for a_ref, w_ref in zip(lhs[1:], ws[1:]):
        acc = acc + _dot(a_ref[...], w_ref[...])
    o_ref[...] = x_ref[...] + gate_ref[...] * acc


def _gated_matmul(lhs_list, w, x, mod3, gate_idx, *, seq, tm, tn):
    m, n = x.shape
    n_lhs = len(lhs_list)
    tiles_per_seq = seq // tm
    tn_per_d = n // tn
    in_specs, row0 = [], 0
    for a in lhs_list:
        in_specs.append(pl.BlockSpec((tm, a.shape[1]), lambda i, j: (i, 0)))
    for a in lhs_list:
        kp = a.shape[1]
        assert row0 % kp == 0
        in_specs.append(pl.BlockSpec((kp, tn), functools.partial(lambda i, j, r: (r, j), r=row0 // kp)))
        row0 += kp
    assert row0 == w.shape[0]
    in_specs.append(pl.BlockSpec((tm, tn), lambda i, j: (i, j)))
    in_specs.append(pl.BlockSpec((None, 1, tn), lambda i, j: (i // tiles_per_seq, 0, gate_idx * tn_per_d + j)))
    return pl.pallas_call(
        functools.partial(_gated_matmul_kernel, n_lhs=n_lhs),
        grid=(m // tm, n // tn),
        in_specs=in_specs,
        out_specs=pl.BlockSpec((tm, tn), lambda i, j: (i, j)),
        out_shape=jax.ShapeDtypeStruct((m, n), F32),
        compiler_params=_cparams(2),
        name="gated_matmul",
    )(*lhs_list, *([w] * n_lhs), x, mod3)


def _split3_bf16(x):
    hi = x.astype(BF16)
    r1 = x - hi.astype(F32)
    mid = r1.astype(BF16)
    lo = (r1 - mid.astype(F32)).astype(BF16)
    return hi, mid, lo


def _decay_cumsum_kernel(f_ref, b_ref, o_ref, *, n_blocks):
    row = lax.broadcasted_iota(jnp.int32, (LANES, LANES), 0)
    col = lax.broadcasted_iota(jnp.int32, (LANES, LANES), 1)
    tri = (col <= row).astype(BF16)

    def body(i, carry):
        rows = pl.ds(pl.multiple_of(i * LANES, LANES), LANES)
        z = f_ref[rows, :] + b_ref[...]
        log_f = jnp.minimum(z, 0.0) - jnp.log1p(jnp.exp(-jnp.abs(z)))
        hi, mid, lo = _split3_bf16(log_f)
        pre = (_dot(tri, hi) + _dot(tri, mid)) + _dot(tri, lo) + carry
        o_ref[rows, :] = pre
        return pre[LANES - 1:LANES, :]

    lax.fori_loop(0, n_blocks, body, jnp.zeros((1, LANES), F32))


def _decay_cumsum(proj_b, f_col_block, b_forget_pad, *, batch, seq):
    return pl.pallas_call(
        functools.partial(_decay_cumsum_kernel, n_blocks=seq // LANES),
        grid=(batch,),
        in_specs=[
            pl.BlockSpec((seq, LANES), lambda b: (b, f_col_block)),
            pl.BlockSpec((1, LANES), lambda b: (0, 0)),
        ],
        out_specs=pl.BlockSpec((None, seq, LANES), lambda b: (b, 0, 0)),
        out_shape=jax.ShapeDtypeStruct((batch, seq, LANES), F32),
        compiler_params=_cparams(1),
        name="decay_cumsum",
    )(proj_b, b_forget_pad)


HEADS_PER_STEP = 2


def _head_cols(u):
    return slice(u * HEAD_DIM, (u + 1) * HEAD_DIM)


def _flash_init(m_scr, l_scr, acc_scr):
    m_scr[...] = jnp.full(m_scr.shape, NEG, F32)
    l_scr[...] = jnp.zeros(l_scr.shape, F32)
    acc_scr[...] = jnp.zeros(acc_scr.shape, F32)


def _flash_step(s, v, m_scr, l_scr, acc_scr):
    m_prev = m_scr[...]
    m_new = jnp.maximum(m_prev, jnp.max(s, axis=-1, keepdims=True))
    alpha = jnp.exp(m_prev - m_new)
    p = jnp.exp(s - pltpu.repeat(m_new, s.shape[1] // LANES, 1))
    l_scr[...] = alpha * l_scr[...] + jnp.sum(p, axis=-1, keepdims=True)
    acc_scr[...] = alpha * acc_scr[...] + _dot(p.astype(BF16), v)
    m_scr[...] = m_new


def _causal_blocks(qi, tq, tk, step):
    n_full = (qi * tq) // tk

    def body(kb, carry):
        step(kb, None)
        return carry

    lax.fori_loop(0, n_full, body, 0)
    return n_full


N_SPLIT = 3


def _fox_kernel(q_ref, k_ref, v_ref, fcum_ref, o_ref, qaug_scr, kaug_scr, m_scr, l_scr, acc_scr, *, tq, tk, seq):
    hp = pl.program_id(1)
    qi = pl.program_id(2)

    @pl.when(qi == 0)
    def _():
        chunk = 512

        def build(c, carry):
            rows = pl.ds(pl.multiple_of(c * chunk, chunk), chunk)
            f = fcum_ref[rows, :]
            lane = lax.broadcasted_iota(jnp.int32, f.shape, 1)
            ones = jnp.where(lane < 2 * N_SPLIT, 1.0, 0.0)
            for u in range(HEADS_PER_STEP):
                f_h = jnp.sum(jnp.where(lane == hp * HEADS_PER_STEP + u, f, 0.0), axis=-1, keepdims=True)
                qa, ka = ones, ones
                for j, piece in enumerate(_split3_bf16(f_h)):
                    piece = piece.astype(F32)
                    qa = jnp.where(lane == j, piece, qa)
                    ka = jnp.where(lane == N_SPLIT + j, -piece, ka)
                qaug_scr[u, rows, :] = qa.astype(BF16)
                kaug_scr[u, rows, :] = ka.astype(BF16)
            return carry

        lax.fori_loop(0, seq // chunk, build, 0)

    _flash_init(m_scr, l_scr, acc_scr)
    q_rows = pl.ds(pl.multiple_of(qi * tq, tq), tq)

    def step(kb, mask):
        cols = pl.ds(pl.multiple_of(kb * tk, tk), tk)
        for u in range(HEADS_PER_STEP):
            q2 = jnp.concatenate([q_ref[:, _head_cols(u)], qaug_scr[u, q_rows, :]], axis=1)
            k2 = jnp.concatenate([k_ref[cols, _head_cols(u)], kaug_scr[u, cols, :]], axis=1)
            s = _dot_nt(q2, k2)
            if mask is not None:
                s = jnp.where(mask, s, NEG)
            _flash_step(s, v_ref[cols, _head_cols(u)], m_scr.at[u], l_scr.at[u], acc_scr.at[u])

    n_full = _causal_blocks(qi, tq, tk, step)
    row = lax.broadcasted_iota(jnp.int32, (tq, tk), 0) + (qi * tq - n_full * tk)
    col = lax.broadcasted_iota(jnp.int32, (tq, tk), 1)
    step(n_full, col <= row)
    for u in range(HEADS_PER_STEP):
        o_ref[:, _head_cols(u)] = (acc_scr[u] / l_scr[u]).astype(o_ref.dtype)


def _fox_attention(proj, q_blk0, k_blk0, v_blk0, f_cum, *, batch, seq, tq=512, tk=512):
    nq = seq // tq
    hps = HEADS_PER_STEP
    wide = hps * HEAD_DIM
    assert q_blk0 % hps == 0 and k_blk0 % hps == 0 and v_blk0 % hps == 0 and H_FOX % hps == 0
    return pl.pallas_call(
        functools.partial(_fox_kernel, tq=tq, tk=tk, seq=seq),
        grid=(batch, H_FOX // hps, nq),
        in_specs=[
            pl.BlockSpec((tq, wide), lambda b, h, i: (b * nq + i, q_blk0 // hps + h)),
            pl.BlockSpec((seq, wide), lambda b, h, i: (b, k_blk0 // hps + h)),
            pl.BlockSpec((seq, wide), lambda b, h, i: (b, v_blk0 // hps + h)),
            pl.BlockSpec((None, seq, LANES), lambda b, h, i: (b, 0, 0)),
        ],
        out_specs=pl.BlockSpec((tq, wide), lambda b, h, i: (b * nq + i, h)),
        out_shape=jax.ShapeDtypeStruct((batch * seq, FOX_W), BF16),
        scratch_shapes=[pltpu.VMEM((hps, seq, HEAD_DIM), BF16), pltpu.VMEM((hps, seq, HEAD_DIM), BF16),
                        pltpu.VMEM((hps, tq, LANES), F32), pltpu.VMEM((hps, tq, LANES), F32),
                        pltpu.VMEM((hps, tq, HEAD_DIM), F32)],
        compiler_params=_cparams(3),
        name="fox_attention",
    )(proj, proj, proj, f_cum)


def _diff_kernel(q_ref, k_ref, v_ref, bias_ref, lam_ref, g_ref, o_ref, m_scr, l_scr, acc_scr,
                 *, tq, tk, lam_init):
    qi = pl.program_id(2)
    _flash_init(m_scr, l_scr, acc_scr)
    r_sub, c_sub = tq // LANES, tk // LANES

    def step(kb, mask):
        cols = pl.ds(pl.multiple_of(kb * tk, tk), tk)
        t0 = qi * r_sub - kb * c_sub
        for u in range(HEADS_PER_STEP):
            q = q_ref[:, _head_cols(u)]
            lane = lax.broadcasted_iota(jnp.int32, q.shape, 1)
            zero = jnp.zeros_like(q)
            q2 = jnp.concatenate([jnp.where(lane < DIFF_QK_DIM, q, zero),
                                  jnp.where(lane >= DIFF_QK_DIM, q, zero)], axis=0)
            s = _dot_nt(q2, k_ref[cols, _head_cols(u)])
            bias = jnp.concatenate(
                [jnp.concatenate([bias_ref[u, jnp.maximum(t0 + a - c, 0)] for c in range(c_sub)], axis=1)
                 for a in range(r_sub)], axis=0)
            s = s + jnp.concatenate([bias, bias], axis=0)
            if mask is not None:
                s = jnp.where(mask, s, NEG)
            _flash_step(s, v_ref[cols, _head_cols(u)], m_scr.at[u], l_scr.at[u], acc_scr.at[u])

    n_full = _causal_blocks(qi, tq, tk, step)
    row = lax.broadcasted_iota(jnp.int32, (2 * tq, tk), 0)
    row = jnp.where(row >= tq, row - tq, row) + (qi * tq - n_full * tk)
    col = lax.broadcasted_iota(jnp.int32, (2 * tq, tk), 1)
    step(n_full, col <= row)

    lam_v = lam_ref[...]
    lam = (jnp.exp(jnp.sum(lam_v[0:1] * lam_v[1:2], axis=-1, keepdims=True))
           - jnp.exp(jnp.sum(lam_v[2:3] * lam_v[3:4], axis=-1, keepdims=True)) + lam_init)
    for u in range(HEADS_PER_STEP):
        o_all = acc_scr[u] / l_scr[u]
        o = o_all[:tq] - lam * o_all[tq:]
        y = o * lax.rsqrt(jnp.mean(o * o, axis=-1, keepdims=True) + RMS_EPS)
        o_ref[:, _head_cols(u)] = ((y * g_ref[...]) * (1.0 - lam_init)).astype(o_ref.dtype)


def _diff_attention(proj, q_blk0, k_blk0, v_blk0, bias_tiles, lam_rows, subln_g, *, batch, seq, lam_init,
                    tq=256, tk=512):
    nq = seq // tq
    n_tiles = seq // LANES
    hps = HEADS_PER_STEP
    wide = hps * HEAD_DIM
    assert q_blk0 % hps == 0 and k_blk0 % hps == 0 and v_blk0 % hps == 0 and H_DIFF % hps == 0
    return pl.pallas_call(
        functools.partial(_diff_kernel, tq=tq, tk=tk, lam_init=lam_init),
        grid=(batch, H_DIFF // hps, nq),
        in_specs=[
            pl.BlockSpec((tq, wide), lambda b, h, i: (b * nq + i, q_blk0 // hps + h)),
            pl.BlockSpec((seq, wide), lambda b, h, i: (b, k_blk0 // hps + h)),
            pl.BlockSpec((seq, wide), lambda b, h, i: (b, v_blk0 // hps + h)),
            pl.BlockSpec((hps, n_tiles, LANES, LANES), lambda b, h, i: (h, 0, 0, 0)),
            pl.BlockSpec((4, DIFF_QK_DIM), lambda b, h, i: (0, 0)),
            pl.BlockSpec((1, HEAD_DIM), lambda b, h, i: (0, 0)),
        ],
        out_specs=pl.BlockSpec((tq, wide), lambda b, h, i: (b * nq + i, h)),
        out_shape=jax.ShapeDtypeStruct((batch * seq, DIFF_W), BF16),
        scratch_shapes=[pltpu.VMEM((hps, 2 * tq, LANES), F32), pltpu.VMEM((hps, 2 * tq, LANES), F32),
                        pltpu.VMEM((hps, 2 * tq, HEAD_DIM), F32)],
        compiler_params=_cparams(3),
        name="diff_attention",
    )(proj, proj, proj, bias_tiles, lam_rows, subln_g)


def _dil_kernel(q_ref, k_ref, v_ref, bias_ref, o_ref, o_scr, lse_scr, *, seq):
    blk = DIL_BLK
    for p, (_, dil) in enumerate(DIL_PATTERNS):
        n_blocks = seq // (dil * blk)

        def body(it, carry, p=p, dil=dil, n_blocks=n_blocks):
            r = it // n_blocks
            n = it - r * n_blocks
            cur = pl.ds(r + n * (blk * dil), blk, stride=dil)
            prev = pl.ds(r + jnp.maximum(n - 1, 0) * (blk * dil), blk, stride=dil)
            q = q_ref[cur, :].astype(BF16)
            bias = bias_ref[p]
            s_prev = _dot_nt(q, k_ref[prev, :].astype(BF16)) + bias[:, :blk]
            s_prev = jnp.where(n > 0, s_prev, NEG)
            s_cur = _dot_nt(q, k_ref[cur, :].astype(BF16)) + bias[:, blk:]
            m = jnp.maximum(jnp.max(s_prev, axis=-1, keepdims=True), jnp.max(s_cur, axis=-1, keepdims=True))
            e_prev = jnp.exp(s_prev - m)
            e_cur = jnp.exp(s_cur - m)
            den = jnp.sum(e_prev, axis=-1, keepdims=True) + jnp.sum(e_cur, axis=-1, keepdims=True)
            inv = 1.0 / den
            o = (_dot((e_prev * inv).astype(BF16), v_ref[prev, :].astype(BF16))
                 + _dot((e_cur * inv).astype(BF16), v_ref[cur, :].astype(BF16)))
            o_scr[p, cur, :] = o
            lse_scr[p, cur, :] = jnp.broadcast_to(m + jnp.log(den), (blk, LANES))
            return carry

        lax.fori_loop(0, dil * n_blocks, body, 0, unroll=DIL_UNROLL)

    chunk = 512

    def mix(i, carry):
        rows = pl.ds(pl.multiple_of(i * chunk, chunk), chunk)
        lse = [lse_scr[p, rows, :] for p in range(len(DIL_PATTERNS))]
        mx = functools.reduce(jnp.maximum, lse)
        w = [jnp.exp(x - mx) for x in lse]
        inv = 1.0 / functools.reduce(lambda a, b: a + b, w)
        acc = (w[0] * inv) * o_scr[0, rows, :]
        for p in range(1, len(DIL_PATTERNS)):
            acc = acc + (w[p] * inv) * o_scr[p, rows, :]
        o_ref[rows, :] = acc.astype(o_ref.dtype)
        return carry

    lax.fori_loop(0, seq // chunk, mix, 0)


def _dil_attention(proj, q_blk0, k_blk0, v_blk0, bias_tiles, *, batch, seq):
    n_pat = len(DIL_PATTERNS)
    return pl.pallas_call(
        functools.partial(_dil_kernel, seq=seq),
        grid=(batch, H_DIL),
        in_specs=[
            pl.BlockSpec((seq, HEAD_DIM), lambda b, h: (b, q_blk0 + h)),
            pl.BlockSpec((seq, HEAD_DIM), lambda b, h: (b, k_blk0 + h)),
            pl.BlockSpec((seq, HEAD_DIM), lambda b, h: (b, v_blk0 + h)),
            pl.BlockSpec((None, n_pat, DIL_BLK, 2 * DIL_BLK), lambda b, h: (h, 0, 0, 0)),
        ],
        out_specs=pl.BlockSpec((seq, HEAD_DIM), lambda b, h: (b, h)),
        out_shape=jax.ShapeDtypeStruct((batch * seq, DIL_W), BF16),
        scratch_shapes=[pltpu.VMEM((n_pat, seq, HEAD_DIM), F32), pltpu.VMEM((n_pat, seq, LANES), F32)],
        compiler_params=_cparams(2),
        name="dil_attention",
    )(proj, proj, proj, bias_tiles)


def _rmsnorm_kernel(x_ref, g_ref, o_ref):
    x = x_ref[...]
    o_ref[...] = (x * lax.rsqrt(jnp.mean(x * x, axis=-1, keepdims=True) + RMS_EPS)) * g_ref[...]


def _rmsnorm(x, g, tm=512):
    m, d = x.shape
    return pl.pallas_call(
        _rmsnorm_kernel,
        grid=(m // tm,),
        in_specs=[pl.BlockSpec((tm, d), lambda i: (i, 0)), pl.BlockSpec((1, d), lambda i: (0, 0))],
        out_specs=pl.BlockSpec((tm, d), lambda i: (i, 0)),
        out_shape=jax.ShapeDtypeStruct((m, d), F32),
        compiler_params=_cparams(1),
        name="final_rmsnorm",
    )(x, g)


def _split_w_in(w_in_l):
    fox_qkv = 3 * FOX_W
    dil_qkv = 3 * DIL_W
    o_f = fox_qkv
    o_dil = o_f + H_FOX
    o_dqk = o_dil + dil_qkv
    w_a = jnp.concatenate([w_in_l[:, :fox_qkv], w_in_l[:, o_dqk:]], axis=1).astype(BF16)
    pad = jnp.zeros((w_in_l.shape[0], 2 * LANES - H_FOX), w_in_l.dtype)
    w_b = jnp.concatenate([w_in_l[:, o_dil:o_dqk], w_in_l[:, o_f:o_dil], pad], axis=1).astype(BF16)
    return w_a, w_b


def kernel(x, c, w_ada, b_ada, norm1_g, w_in, b_forget, lambda_q1, lambda_k1, lambda_q2, lambda_k2,
           diff_subln_g, w_out, norm2_g, w_mlp1, w_mlp2, rel_table, final_norm_g):
    batch, seq, d = x.shape
    depth = w_ada.shape[0]
    assert d == D_MODEL and seq % (DIL_PATTERNS[-1][1] * DIL_BLK) == 0

    c_rows = jnp.zeros((SUBLANES, d), F32).at[:batch].set(c)
    mod = _ada_mod(c_rows, w_ada, b_ada)[:, :batch]
    diff_bias = _diff_bias_tiles(rel_table, seq)
    dil_bias = _dil_bias_tiles(rel_table)

    scale_a = jnp.ones((1, 3 * FOX_W + 3 * DIFF_W), F32)
    scale_a = scale_a.at[:, :FOX_W].set(HEAD_DIM ** -0.5)
    scale_a = scale_a.at[:, 3 * FOX_W:3 * FOX_W + DIFF_W].set(DIFF_QK_DIM ** -0.5)
    scale_b = jnp.ones((1, 3 * DIL_W + 2 * LANES), F32).at[:, :DIL_W].set(HEAD_DIM ** -0.5)
    ones_ff = jnp.ones((1, D_FF), F32)

    xf = x.reshape(batch * seq, d)
    for l in range(depth):
        mod3 = mod[l].reshape(batch, 1, N_MOD * d)
        w_a, w_b = _split_w_in(w_in[l])
        proj_a = _norm_matmul(xf, norm1_g[l].reshape(1, d), mod3, 0, 1, w_a, scale_a, BF16,
                              seq=seq, tm=512, tn=768)
        proj_b = _norm_matmul(xf, norm1_g[l].reshape(1, d), mod3, 0, 1, w_b, scale_b, F32,
                              seq=seq, tm=512, tn=512)

        b_f = jnp.zeros((1, LANES), F32).at[0, :H_FOX].set(b_forget[l])
        f_cum = _decay_cumsum(proj_b, 3 * DIL_W // LANES, b_f, batch=batch, seq=seq)
        y_a = _fox_attention(proj_a, 0, H_FOX, 2 * H_FOX, f_cum, batch=batch, seq=seq)

        y_b = _dil_attention(proj_b, 0, H_DIL, 2 * H_DIL, dil_bias, batch=batch, seq=seq)

        lam_init = 0.8 - 0.6 * math.exp(-0.3 * l)
        lam_rows = jnp.stack([lambda_q1[l], lambda_k1[l], lambda_q2[l], lambda_k2[l]]).astype(F32)
        c0 = 3 * H_FOX
        y_c = _diff_attention(proj_a, c0, c0 + H_DIFF, c0 + 2 * H_DIFF, diff_bias, lam_rows,
                              diff_subln_g[l].reshape(1, HEAD_DIM), batch=batch, seq=seq, lam_init=lam_init)

        xf = _gated_matmul([y_a, y_b, y_c], w_out[l].astype(BF16), xf, mod3, 2, seq=seq, tm=512, tn=1024)

        hidden = _norm_matmul(xf, norm2_g[l].reshape(1, d), mod3, 3, 4, w_mlp1[l].astype(BF16), ones_ff, BF16,
                              seq=seq, tm=512, tn=1024, relu2=True)
        xf = _gated_matmul([hidden], w_mlp2[l].astype(BF16), xf, mod3, 5, seq=seq, tm=512, tn=512)

    out = _rmsnorm(xf, final_norm_g.reshape(1, d))
    return out.reshape(batch, seq, d)
```

```python
import functools
import math

import numpy as np
import jax
import jax.numpy as jnp
from jax import lax
from jax.experimental import pallas as pl
from jax.experimental.pallas import tpu as pltpu

F32 = jnp.float32
BF16 = jnp.bfloat16

D_MODEL = 2048
HEAD_DIM = 128
N_HEADS = D_MODEL // HEAD_DIM
H_DIFF = N_HEADS // 4
H_FOX = (N_HEADS - H_DIFF) // 2
H_DIL = N_HEADS - H_FOX - H_DIFF
DIFF_QK_DIM = HEAD_DIM // 2
DIL_PATTERNS = ((128, 1), (512, 4), (2048, 16))
DIL_BLK = 128
DIL_UNROLL = 4
NUM_BUCKETS = 32
MAX_DISTANCE = 2048
D_FF = 4 * D_MODEL
N_MOD = 6
RMS_EPS = 1e-6
FOX_W = H_FOX * HEAD_DIM
DIL_W = H_DIL * HEAD_DIM
DIFF_W = H_DIFF * HEAD_DIM
NEG = -1e30

LANES = 128
SUBLANES = 8
VMEM_LIMIT_BYTES = 56 * 1024 * 1024


def _t5_thresholds():
    max_exact = NUM_BUCKETS // 2
    n = np.arange(max_exact, 2 * MAX_DISTANCE, dtype=np.float64)
    val = np.log(n / max_exact) / math.log(MAX_DISTANCE / max_exact) * (NUM_BUCKETS - max_exact)
    inside = n < MAX_DISTANCE
    frac = np.abs(val - np.round(val))
    assert np.all((frac[inside] > 1e-5) | (frac[inside] == 0.0))
    large = np.minimum(max_exact + np.floor(val).astype(np.int64), NUM_BUCKETS - 1)
    thr = list(range(max_exact))
    for b in range(max_exact, NUM_BUCKETS):
        thr.append(int(n[np.nonzero(large >= b)[0][0]]))
    return tuple(thr)


T5_THRESHOLDS = _t5_thresholds()


def _cparams(n_grid):
    return pltpu.CompilerParams(dimension_semantics=("arbitrary",) * n_grid,
                                vmem_limit_bytes=VMEM_LIMIT_BYTES)


def _dot(a, b):
    return jnp.dot(a, b, preferred_element_type=F32)


def _dot_nt(a, b):
    return lax.dot_general(a, b, (((1,), (1,)), ((), ())), preferred_element_type=F32)


def _ada_kernel(c_ref, w_ref, b_ref, o_ref):
    o_ref[...] = _dot(c_ref[...].astype(BF16), w_ref[...].astype(BF16)) + b_ref[...]


def _ada_mod(c, w_ada, b_ada, tn=512):
    depth, d, n = w_ada.shape
    b = c.shape[0]
    return pl.pallas_call(
        _ada_kernel,
        grid=(depth, n // tn),
        in_specs=[
            pl.BlockSpec((b, d), lambda l, j: (0, 0)),
            pl.BlockSpec((None, d, tn), lambda l, j: (l, 0, j)),
            pl.BlockSpec((None, 1, tn), lambda l, j: (l, 0, j)),
        ],
        out_specs=pl.BlockSpec((None, b, tn), lambda l, j: (l, 0, j)),
        out_shape=jax.ShapeDtypeStruct((depth, b, n), F32),
        compiler_params=_cparams(2),
        name="ada_mod",
    )(c, w_ada, b_ada.reshape(depth, 1, n))


def _t5_bias_of(n, table_ref, col):
    val = jnp.full(n.shape, table_ref[0, col], F32)
    for b in range(1, NUM_BUCKETS):
        val = jnp.where(n >= T5_THRESHOLDS[b], table_ref[b, col], val)
    return val


def _diff_bias_kernel(table_ref, o_ref, *, n_tiles):
    h = pl.program_id(0)
    row = lax.broadcasted_iota(jnp.int32, (LANES, LANES), 0)
    col = lax.broadcasted_iota(jnp.int32, (LANES, LANES), 1)

    def body(t, carry):
        o_ref[t] = _t5_bias_of(t * LANES + row - col, table_ref, H_DIL + h)
        return carry

    lax.fori_loop(0, n_tiles, body, 0)


def _diff_bias_tiles(rel_table, seq):
    n_tiles = seq // LANES
    return pl.pallas_call(
        functools.partial(_diff_bias_kernel, n_tiles=n_tiles),
        grid=(H_DIFF,),
        in_specs=[pl.BlockSpec(memory_space=pltpu.SMEM)],
        out_specs=pl.BlockSpec((None, n_tiles, LANES, LANES), lambda h: (h, 0, 0, 0)),
        out_shape=jax.ShapeDtypeStruct((H_DIFF, n_tiles, LANES, LANES), F32),
        compiler_params=_cparams(1),
        name="diff_bias_tiles",
    )(rel_table)


def _dil_bias_kernel(table_ref, o_ref):
    h = pl.program_id(0)
    qi = lax.broadcasted_iota(jnp.int32, (DIL_BLK, 2 * DIL_BLK), 0)
    ki = lax.broadcasted_iota(jnp.int32, (DIL_BLK, 2 * DIL_BLK), 1)
    dist = DIL_BLK + qi - ki
    valid = (dist >= 0) & (dist <= DIL_BLK)
    for p, (_, dil) in enumerate(DIL_PATTERNS):
        o_ref[p] = jnp.where(valid, _t5_bias_of(dist * dil, table_ref, h), NEG)


def _dil_bias_tiles(rel_table):
    n_pat = len(DIL_PATTERNS)
    return pl.pallas_call(
        _dil_bias_kernel,
        grid=(H_DIL,),
        in_specs=[pl.BlockSpec(memory_space=pltpu.SMEM)],
        out_specs=pl.BlockSpec((None, n_pat, DIL_BLK, 2 * DIL_BLK), lambda h: (h, 0, 0, 0)),
        out_shape=jax.ShapeDtypeStruct((H_DIL, n_pat, DIL_BLK, 2 * DIL_BLK), F32),
        compiler_params=_cparams(1),
        name="dil_bias_tiles",
    )(rel_table)


def _norm_matmul_kernel(x_ref, g_ref, sh_ref, sc_ref, w_ref, cs_ref, o_ref, h_scr, *, relu2):
    @pl.when(pl.program_id(1) == 0)
    def _():
        x = x_ref[...]
        y = x * lax.rsqrt(jnp.mean(x * x, axis=-1, keepdims=True) + RMS_EPS)
        y = y * g_ref[...]
        h_scr[...] = (y * (1.0 + sc_ref[...]) + sh_ref[...]).astype(BF16)

    acc = _dot(h_scr[...], w_ref[...])
    if relu2:
        acc = jnp.square(jnp.maximum(acc, 0.0))
    o_ref[...] = (acc * cs_ref[...]).astype(o_ref.dtype)


def _norm_matmul(x, g, mod3, shift_idx, scale_idx, w, layer, col_scale, out_dtype, *, seq, tm, tn, relu2=False):
    m, d = x.shape
    n = w.shape[2]
    tiles_per_seq = seq // tm
    return pl.pallas_call(
        functools.partial(_norm_matmul_kernel, relu2=relu2),
        grid=(m // tm, n // tn),
        in_specs=[
            pl.BlockSpec((tm, d), lambda i, j: (i, 0)),
            pl.BlockSpec((None, 1, d), lambda i, j: (layer, 0, 0)),
            pl.BlockSpec((None, 1, d), lambda i, j: (i // tiles_per_seq, 0, shift_idx)),
            pl.BlockSpec((None, 1, d), lambda i, j: (i // tiles_per_seq, 0, scale_idx)),
            pl.BlockSpec((None, d, tn), lambda i, j: (layer, 0, j)),
            pl.BlockSpec((1, tn), lambda i, j: (0, j)),
        ],
        out_specs=pl.BlockSpec((tm, tn), lambda i, j: (i, j)),
        out_shape=jax.ShapeDtypeStruct((m, n), out_dtype),
        scratch_shapes=[pltpu.VMEM((tm, d), BF16)],
        compiler_params=_cparams(2),
        name="norm_matmul_relu2" if relu2 else "norm_matmul",
    )(x, g, mod3, mod3, w, col_scale)


def _gated_matmul_kernel(*refs, n_lhs):
    lhs = refs[:n_lhs]
    ws = refs[n_lhs:2 * n_lhs]
    x_ref, gate_ref, o_ref = refs[2 * n_lhs:]
    acc = _dot(lhs[0][...], ws[0][...])
    for a_ref, w_ref in zip(lhs[1:], ws[1:]):
        acc = acc + _dot(a_ref[...], w_ref[...])
    o_ref[...] = x_ref[...] + gate_ref[...] * acc


def _gated_matmul(lhs_list, w, layer, x, mod3, gate_idx, *, seq, tm, tn):
    m, n = x.shape
    n_lhs = len(lhs_list)
    tiles_per_seq = seq // tm
    tn_per_d = n // tn
    in_specs, row0 = [], 0
    for a in lhs_list:
        in_specs.append(pl.BlockSpec((tm, a.shape[1]), lambda i, j: (i, 0)))
    for a in lhs_list:
        kp = a.shape[1]
        assert row0 % kp == 0
        in_specs.append(pl.BlockSpec((None, kp, tn), functools.partial(lambda i, j, r: (layer, r, j), r=row0 // kp)))
        row0 += kp
    assert row0 == w.shape[1]
    in_specs.append(pl.BlockSpec((tm, tn), lambda i, j: (i, j)))
    in_specs.append(pl.BlockSpec((None, 1, tn), lambda i, j: (i // tiles_per_seq, 0, gate_idx * tn_per_d + j)))
    return pl.pallas_call(
        functools.partial(_gated_matmul_kernel, n_lhs=n_lhs),
        grid=(m // tm, n // tn),
        in_specs=in_specs,
        out_specs=pl.BlockSpec((tm, tn), lambda i, j: (i, j)),
        out_shape=jax.ShapeDtypeStruct((m, n), F32),
        compiler_params=_cparams(2),
        name="gated_matmul",
    )(*lhs_list, *([w] * n_lhs), x, mod3)


def _split3_bf16(x):
    hi = x.astype(BF16)
    r1 = x - hi.astype(F32)
    mid = r1.astype(BF16)
    lo = (r1 - mid.astype(F32)).astype(BF16)
    return hi, mid, lo


def _decay_cumsum_kernel(f_ref, b_ref, o_ref, *, n_blocks):
    row = lax.broadcasted_iota(jnp.int32, (LANES, LANES), 0)
    col = lax.broadcasted_iota(jnp.int32, (LANES, LANES), 1)
    tri = (col <= row).astype(BF16)

    def body(i, carry):
        rows = pl.ds(pl.multiple_of(i * LANES, LANES), LANES)
        z = f_ref[rows, :] + b_ref[...]
        log_f = jnp.minimum(z, 0.0) - jnp.log1p(jnp.exp(-jnp.abs(z)))
        hi, mid, lo = _split3_bf16(log_f)
        pre = (_dot(tri, hi) + _dot(tri, mid)) + _dot(tri, lo) + carry
        o_ref[rows, :] = pre
        return pre[LANES - 1:LANES, :]

    lax.fori_loop(0, n_blocks, body, jnp.zeros((1, LANES), F32))


def _decay_cumsum(proj_b, f_col_block, b_forget_pad, *, batch, seq):
    return pl.pallas_call(
        functools.partial(_decay_cumsum_kernel, n_blocks=seq // LANES),
        grid=(batch,),
        in_specs=[
            pl.BlockSpec((seq, LANES), lambda b: (b, f_col_block)),
            pl.BlockSpec((1, LANES), lambda b: (0, 0)),
        ],
        out_specs=pl.BlockSpec((None, seq, LANES), lambda b: (b, 0, 0)),
        out_shape=jax.ShapeDtypeStruct((batch, seq, LANES), F32),
        compiler_params=_cparams(1),
        name="decay_cumsum",
    )(proj_b, b_forget_pad)


HEADS_PER_STEP = 2


def _head_cols(u):
    return slice(u * HEAD_DIM, (u + 1) * HEAD_DIM)


def _flash_init(m_scr, l_scr, acc_scr):
    m_scr[...] = jnp.full(m_scr.shape, NEG, F32)
    l_scr[...] = jnp.zeros(l_scr.shape, F32)
    acc_scr[...] = jnp.zeros(acc_scr.shape, F32)


def _flash_step(logit_fns, value_fns, mask, m_scr, l_scr, acc_scr):
    scores = [fn() for fn in logit_fns]
    probs = []
    for u, s in enumerate(scores):
        if mask is not None:
            s = jnp.where(mask, s, NEG)
        m_prev = m_scr[u]
        m_new = jnp.maximum(m_prev, jnp.max(s, axis=-1, keepdims=True))
        alpha = jnp.exp(m_prev - m_new)
        p = jnp.exp(s - jnp.tile(m_new, (1, s.shape[1] // LANES)))
        l_scr[u] = alpha * l_scr[u] + jnp.sum(p, axis=-1, keepdims=True)
        m_scr[u] = m_new
        probs.append((alpha, p.astype(BF16)))
    for u, (alpha, p) in enumerate(probs):
        acc_scr[u] = alpha * acc_scr[u] + _dot(p, value_fns[u]())


def _causal_blocks(qi, tq, tk, step):
    n_full = (qi * tq) // tk

    def body(kb, carry):
        step(kb, None)
        return carry

    lax.fori_loop(0, n_full, body, 0)
    return n_full


N_SPLIT = 3


def _fox_kernel(q_ref, k_ref, v_ref, fcum_ref, o_ref, qaug_scr, kaug_scr, m_scr, l_scr, acc_scr, *, tq, tk, seq):
    hp = pl.program_id(1)
    qi = pl.program_id(2)

    @pl.when(qi == 0)
    def _():
        chunk = 512

        def build(c, carry):
            rows = pl.ds(pl.multiple_of(c * chunk, chunk), chunk)
            f = fcum_ref[rows, :]
            lane = lax.broadcasted_iota(jnp.int32, f.shape, 1)
            ones = jnp.where(lane < 2 * N_SPLIT, 1.0, 0.0)
            for u in range(HEADS_PER_STEP):
                f_h = jnp.sum(jnp.where(lane == hp * HEADS_PER_STEP + u, f, 0.0), axis=-1, keepdims=True)
                qa, ka = ones, ones
                for j, piece in enumerate(_split3_bf16(f_h)):
                    piece = piece.astype(F32)
                    qa = jnp.where(lane == j, piece, qa)
                    ka = jnp.where(lane == N_SPLIT + j, -piece, ka)
                qaug_scr[u, rows, :] = qa.astype(BF16)
                kaug_scr[u, rows, :] = ka.astype(BF16)
            return carry

        lax.fori_loop(0, seq // chunk, build, 0)

    _flash_init(m_scr, l_scr, acc_scr)
    q_rows = pl.ds(pl.multiple_of(qi * tq, tq), tq)

    def step(kb, mask):
        cols = pl.ds(pl.multiple_of(kb * tk, tk), tk)

        def logits(u):
            q2 = jnp.concatenate([q_ref[:, _head_cols(u)], qaug_scr[u, q_rows, :]], axis=1)
            k2 = jnp.concatenate([k_ref[cols, _head_cols(u)], kaug_scr[u, cols, :]], axis=1)
            return _dot_nt(q2, k2)

        heads = range(HEADS_PER_STEP)
        _flash_step([functools.partial(logits, u) for u in heads],
                    [functools.partial(lambda u: v_ref[cols, _head_cols(u)], u) for u in heads],
                    mask, m_scr, l_scr, acc_scr)

    n_full = _causal_blocks(qi, tq, tk, step)
    row = lax.broadcasted_iota(jnp.int32, (tq, tk), 0) + (qi * tq - n_full * tk)
    col = lax.broadcasted_iota(jnp.int32, (tq, tk), 1)
    step(n_full, col <= row)
    for u in range(HEADS_PER_STEP):
        o_ref[:, _head_cols(u)] = (acc_scr[u] / l_scr[u]).astype(o_ref.dtype)


def _fox_attention(proj, q_blk0, k_blk0, v_blk0, f_cum, *, batch, seq, tq=512, tk=512):
    nq = seq // tq
    hps = HEADS_PER_STEP
    wide = hps * HEAD_DIM
    assert q_blk0 % hps == 0 and k_blk0 % hps == 0 and v_blk0 % hps == 0 and H_FOX % hps == 0
    return pl.pallas_call(
        functools.partial(_fox_kernel, tq=tq, tk=tk, seq=seq),
        grid=(batch, H_FOX // hps, nq),
        in_specs=[
            pl.BlockSpec((tq, wide), lambda b, h, i: (b * nq + i, q_blk0 // hps + h)),
            pl.BlockSpec((seq, wide), lambda b, h, i: (b, k_blk0 // hps + h)),
            pl.BlockSpec((seq, wide), lambda b, h, i: (b, v_blk0 // hps + h)),
            pl.BlockSpec((None, seq, LANES), lambda b, h, i: (b, 0, 0)),
        ],
        out_specs=pl.BlockSpec((tq, wide), lambda b, h, i: (b * nq + i, h)),
        out_shape=jax.ShapeDtypeStruct((batch * seq, FOX_W), BF16),
        scratch_shapes=[pltpu.VMEM((hps, seq, HEAD_DIM), BF16), pltpu.VMEM((hps, seq, HEAD_DIM), BF16),
                        pltpu.VMEM((hps, tq, LANES), F32), pltpu.VMEM((hps, tq, LANES), F32),
                        pltpu.VMEM((hps, tq, HEAD_DIM), F32)],
        compiler_params=_cparams(3),
        name="fox_attention",
    )(proj, proj, proj, f_cum)


def _diff_kernel(q_ref, k_ref, v_ref, bias_ref, lam_ref, g_ref, o_ref, m_scr, l_scr, acc_scr,
                 *, tq, tk, lam_init):
    qi = pl.program_id(2)
    _flash_init(m_scr, l_scr, acc_scr)
    r_sub, c_sub = tq // LANES, tk // LANES

    def step(kb, mask):
        cols = pl.ds(pl.multiple_of(kb * tk, tk), tk)
        t0 = qi * r_sub - kb * c_sub

        def logits(u):
            q = q_ref[:, _head_cols(u)]
            lane = lax.broadcasted_iota(jnp.int32, q.shape, 1)
            zero = jnp.zeros_like(q)
            q2 = jnp.concatenate([jnp.where(lane < DIFF_QK_DIM, q, zero),
                                  jnp.where(lane >= DIFF_QK_DIM, q, zero)], axis=0)
            s = _dot_nt(q2, k_ref[cols, _head_cols(u)])
            bias = jnp.concatenate(
                [jnp.concatenate([bias_ref[u, jnp.maximum(t0 + a - c, 0)] for c in range(c_sub)], axis=1)
                 for a in range(r_sub)], axis=0)
            return s + jnp.concatenate([bias, bias], axis=0)

        heads = range(HEADS_PER_STEP)
        _flash_step([functools.partial(logits, u) for u in heads],
                    [functools.partial(lambda u: v_ref[cols, _head_cols(u)], u) for u in heads],
                    mask, m_scr, l_scr, acc_scr)

    n_full = _causal_blocks(qi, tq, tk, step)
    row = lax.broadcasted_iota(jnp.int32, (2 * tq, tk), 0)
    row = jnp.where(row >= tq, row - tq, row) + (qi * tq - n_full * tk)
    col = lax.broadcasted_iota(jnp.int32, (2 * tq, tk), 1)
    step(n_full, col <= row)

    lam_v = lam_ref[...]
    lam = (jnp.exp(jnp.sum(lam_v[0:1] * lam_v[1:2], axis=-1, keepdims=True))
           - jnp.exp(jnp.sum(lam_v[2:3] * lam_v[3:4], axis=-1, keepdims=True)) + lam_init)
    for u in range(HEADS_PER_STEP):
        o_all = acc_scr[u] / l_scr[u]
        o = o_all[:tq] - lam * o_all[tq:]
        y = o * lax.rsqrt(jnp.mean(o * o, axis=-1, keepdims=True) + RMS_EPS)
        o_ref[:, _head_cols(u)] = ((y * g_ref[...]) * (1.0 - lam_init)).astype(o_ref.dtype)


def _diff_attention(proj, q_blk0, k_blk0, v_blk0, bias_tiles, lam_rows, subln_g, *, batch, seq, lam_init,
                    tq=256, tk=512):
    nq = seq // tq
    n_tiles = seq // LANES
    hps = HEADS_PER_STEP
    wide = hps * HEAD_DIM
    assert q_blk0 % hps == 0 and k_blk0 % hps == 0 and v_blk0 % hps == 0 and H_DIFF % hps == 0
    return pl.pallas_call(
        functools.partial(_diff_kernel, tq=tq, tk=tk, lam_init=lam_init),
        grid=(batch, H_DIFF // hps, nq),
        in_specs=[
            pl.BlockSpec((tq, wide), lambda b, h, i: (b * nq + i, q_blk0 // hps + h)),
            pl.BlockSpec((seq, wide), lambda b, h, i: (b, k_blk0 // hps + h)),
            pl.BlockSpec((seq, wide), lambda b, h, i: (b, v_blk0 // hps + h)),
            pl.BlockSpec((hps, n_tiles, LANES, LANES), lambda b, h, i: (h, 0, 0, 0)),
            pl.BlockSpec((4, DIFF_QK_DIM), lambda b, h, i: (0, 0)),
            pl.BlockSpec((1, HEAD_DIM), lambda b, h, i: (0, 0)),
        ],
        out_specs=pl.BlockSpec((tq, wide), lambda b, h, i: (b * nq + i, h)),
        out_shape=jax.ShapeDtypeStruct((batch * seq, DIFF_W), BF16),
        scratch_shapes=[pltpu.VMEM((hps, 2 * tq, LANES), F32), pltpu.VMEM((hps, 2 * tq, LANES), F32),
                        pltpu.VMEM((hps, 2 * tq, HEAD_DIM), F32)],
        compiler_params=_cparams(3),
        name="diff_attention",
    )(proj, proj, proj, bias_tiles, lam_rows, subln_g)


def _dil_kernel(q_ref, k_ref, v_ref, bias_ref, o_ref, o_scr, lse_scr, *, seq):
    blk = DIL_BLK
    for p, (_, dil) in enumerate(DIL_PATTERNS):
        n_blocks = seq // (dil * blk)

        def body(it, carry, p=p, dil=dil, n_blocks=n_blocks):
            bias = bias_ref[p]
            scores = []
            for u in range(DIL_UNROLL):
                idx = it * DIL_UNROLL + u
                r = idx // n_blocks
                n = idx - r * n_blocks
                cur = pl.ds(r + n * (blk * dil), blk, stride=dil)
                prev = pl.ds(r + jnp.maximum(n - 1, 0) * (blk * dil), blk, stride=dil)
                q = q_ref[cur, :].astype(BF16)
                s_prev = _dot_nt(q, k_ref[prev, :].astype(BF16)) + bias[:, :blk]
                s_prev = jnp.where(n > 0, s_prev, NEG)
                s_cur = _dot_nt(q, k_ref[cur, :].astype(BF16)) + bias[:, blk:]
                scores.append((cur, prev, s_prev, s_cur))
            probs = []
            for cur, prev, s_prev, s_cur in scores:
                m = jnp.maximum(jnp.max(s_prev, axis=-1, keepdims=True), jnp.max(s_cur, axis=-1, keepdims=True))
                e_prev = jnp.exp(s_prev - m)
                e_cur = jnp.exp(s_cur - m)
                den = jnp.sum(e_prev, axis=-1, keepdims=True) + jnp.sum(e_cur, axis=-1, keepdims=True)
                inv = 1.0 / den
                probs.append((cur, prev, (e_prev * inv).astype(BF16), (e_cur * inv).astype(BF16), m + jnp.log(den)))
            for cur, prev, p_prev, p_cur, lse in probs:
                o = _dot(p_prev, v_ref[prev, :].astype(BF16)) + _dot(p_cur, v_ref[cur, :].astype(BF16))
                o_scr[p, cur, :] = o
                lse_scr[p, cur, :] = jnp.broadcast_to(lse, (blk, LANES))
            return carry

        lax.fori_loop(0, dil * n_blocks // DIL_UNROLL, body, 0)

    chunk = 512

    def mix(i, carry):
        rows = pl.ds(pl.multiple_of(i * chunk, chunk), chunk)
        lse = [lse_scr[p, rows, :] for p in range(len(DIL_PATTERNS))]
        mx = functools.reduce(jnp.maximum, lse)
        w = [jnp.exp(x - mx) for x in lse]
        inv = 1.0 / functools.reduce(lambda a, b: a + b, w)
        acc = (w[0] * inv) * o_scr[0, rows, :]
        for p in range(1, len(DIL_PATTERNS)):
            acc = acc + (w[p] * inv) * o_scr[p, rows, :]
        o_ref[rows, :] = acc.astype(o_ref.dtype)
        return carry

    lax.fori_loop(0, seq // chunk, mix, 0)


def _dil_attention(proj, q_blk0, k_blk0, v_blk0, bias_tiles, *, batch, seq):
    n_pat = len(DIL_PATTERNS)
    return pl.pallas_call(
        functools.partial(_dil_kernel, seq=seq),
        grid=(batch, H_DIL),
        in_specs=[
            pl.BlockSpec((seq, HEAD_DIM), lambda b, h: (b, q_blk0 + h)),
            pl.BlockSpec((seq, HEAD_DIM), lambda b, h: (b, k_blk0 + h)),
            pl.BlockSpec((seq, HEAD_DIM), lambda b, h: (b, v_blk0 + h)),
            pl.BlockSpec((None, n_pat, DIL_BLK, 2 * DIL_BLK), lambda b, h: (h, 0, 0, 0)),
        ],
        out_specs=pl.BlockSpec((seq, HEAD_DIM), lambda b, h: (b, h)),
        out_shape=jax.ShapeDtypeStruct((batch * seq, DIL_W), BF16),
        scratch_shapes=[pltpu.VMEM((n_pat, seq, HEAD_DIM), F32), pltpu.VMEM((n_pat, seq, LANES), F32)],
        compiler_params=_cparams(2),
        name="dil_attention",
    )(proj, proj, proj, bias_tiles)


def _rmsnorm_kernel(x_ref, g_ref, o_ref):
    x = x_ref[...]
    o_ref[...] = (x * lax.rsqrt(jnp.mean(x * x, axis=-1, keepdims=True) + RMS_EPS)) * g_ref[...]


def _rmsnorm(x, g, tm=512):
    m, d = x.shape
    return pl.pallas_call(
        _rmsnorm_kernel,
        grid=(m // tm,),
        in_specs=[pl.BlockSpec((tm, d), lambda i: (i, 0)), pl.BlockSpec((1, d), lambda i: (0, 0))],
        out_specs=pl.BlockSpec((tm, d), lambda i: (i, 0)),
        out_shape=jax.ShapeDtypeStruct((m, d), F32),
        compiler_params=_cparams(1),
        name="final_rmsnorm",
    )(x, g)


RELAYOUT_TN = 2 * LANES
W_IN_SHIFT = H_FOX


def _relayout_kernel(src_ref, shift_ref, valid_ref, a_ref, b_ref, o_ref):
    j = pl.program_id(1)
    x = jnp.concatenate([a_ref[...], b_ref[...]], axis=1)
    y = jnp.where(shift_ref[j] == 0, x[:, :RELAYOUT_TN], x[:, W_IN_SHIFT:W_IN_SHIFT + RELAYOUT_TN])
    lane = lax.broadcasted_iota(jnp.int32, y.shape, 1)
    o_ref[...] = jnp.where(lane < valid_ref[j], y, 0.0).astype(o_ref.dtype)


def _relayout_w_in(w_in, tiles):
    depth, d, n = w_in.shape
    src_blk = [c // LANES for c, _ in tiles]
    shift = [c % LANES for c, _ in tiles]
    assert all(s in (0, W_IN_SHIFT) for s in shift) and all(b % 2 == 0 for b in src_blk)
    last_blk = (n - 1) // LANES
    grid_spec = pltpu.PrefetchScalarGridSpec(
        num_scalar_prefetch=3,
        grid=(depth, len(tiles)),
        in_specs=[
            pl.BlockSpec((None, d, RELAYOUT_TN), lambda l, j, src, sh, nv: (l, 0, src[j] // 2)),
            pl.BlockSpec((None, d, LANES), lambda l, j, src, sh, nv: (l, 0, jnp.minimum(src[j] + 2, last_blk))),
        ],
        out_specs=pl.BlockSpec((None, d, RELAYOUT_TN), lambda l, j, src, sh, nv: (l, 0, j)),
    )
    return pl.pallas_call(
        _relayout_kernel,
        grid_spec=grid_spec,
        out_shape=jax.ShapeDtypeStruct((depth, d, len(tiles) * RELAYOUT_TN), BF16),
        compiler_params=_cparams(2),
        name="relayout_w_in",
    )(jnp.asarray(src_blk, jnp.int32), jnp.asarray(shift, jnp.int32),
      jnp.asarray([v for _, v in tiles], jnp.int32), w_in, w_in)


def _split_w_in(w_in):
    fox_qkv = 3 * FOX_W
    dil_qkv = 3 * DIL_W
    o_dil = fox_qkv + H_FOX
    o_diff = o_dil + dil_qkv
    tn = RELAYOUT_TN
    tiles_a = ([(c, tn) for c in range(0, fox_qkv, tn)]
               + [(o_diff + c, tn) for c in range(0, 3 * DIFF_W, tn)])
    tiles_b = [(o_dil + c, tn) for c in range(0, dil_qkv, tn)] + [(fox_qkv, H_FOX)]
    return _relayout_w_in(w_in, tiles_a), _relayout_w_in(w_in, tiles_b)


def kernel(x, c, w_ada, b_ada, norm1_g, w_in, b_forget, lambda_q1, lambda_k1, lambda_q2, lambda_k2,
           diff_subln_g, w_out, norm2_g, w_mlp1, w_mlp2, rel_table, final_norm_g):
    batch, seq, d = x.shape
    depth = w_ada.shape[0]
    assert d == D_MODEL and seq % (DIL_PATTERNS[-1][1] * DIL_BLK) == 0

    c_rows = jnp.zeros((SUBLANES, d), F32).at[:batch].set(c)
    mod = _ada_mod(c_rows, w_ada, b_ada)[:, :batch]
    diff_bias = _diff_bias_tiles(rel_table, seq)
    dil_bias = _dil_bias_tiles(rel_table)

    scale_a = jnp.ones((1, 3 * FOX_W + 3 * DIFF_W), F32)
    scale_a = scale_a.at[:, :FOX_W].set(HEAD_DIM ** -0.5)
    scale_a = scale_a.at[:, 3 * FOX_W:3 * FOX_W + DIFF_W].set(DIFF_QK_DIM ** -0.5)
    scale_b = jnp.ones((1, 3 * DIL_W + 2 * LANES), F32).at[:, :DIL_W].set(HEAD_DIM ** -0.5)
    ones_ff = jnp.ones((1, D_FF), F32)

    w_a, w_b = _split_w_in(w_in)
    w_out_bf, w_mlp1_bf, w_mlp2_bf = w_out.astype(BF16), w_mlp1.astype(BF16), w_mlp2.astype(BF16)
    g1 = norm1_g.reshape(depth, 1, d)
    g2 = norm2_g.reshape(depth, 1, d)

    xf = x.reshape(batch * seq, d)
    for l in range(depth):
        mod3 = mod[l].reshape(batch, 1, N_MOD * d)
        proj_a = _norm_matmul(xf, g1, mod3, 0, 1, w_a, l, scale_a, BF16, seq=seq, tm=512, tn=768)
        proj_b = _norm_matmul(xf, g1, mod3, 0, 1, w_b, l, scale_b, F32, seq=seq, tm=512, tn=512)

        b_f = jnp.zeros((1, LANES), F32).at[0, :H_FOX].set(b_forget[l])
        f_cum = _decay_cumsum(proj_b, 3 * DIL_W // LANES, b_f, batch=batch, seq=seq)
        y_a = _fox_attention(proj_a, 0, H_FOX, 2 * H_FOX, f_cum, batch=batch, seq=seq)

        y_b = _dil_attention(proj_b, 0, H_DIL, 2 * H_DIL, dil_bias, batch=batch, seq=seq)

        lam_init = 0.8 - 0.6 * math.exp(-0.3 * l)
        lam_rows = jnp.stack([lambda_q1[l], lambda_k1[l], lambda_q2[l], lambda_k2[l]]).astype(F32)
        c0 = 3 * H_FOX
        y_c = _diff_attention(proj_a, c0, c0 + H_DIFF, c0 + 2 * H_DIFF, diff_bias, lam_rows,
                              diff_subln_g[l].reshape(1, HEAD_DIM), batch=batch, seq=seq, lam_init=lam_init)

        xf = _gated_matmul([y_a, y_b, y_c], w_out_bf, l, xf, mod3, 2, seq=seq, tm=512, tn=1024)

        hidden = _norm_matmul(xf, g2, mod3, 3, 4, w_mlp1_bf, l, ones_ff, BF16,
                              seq=seq, tm=512, tn=1024, relu2=True)
        xf = _gated_matmul([hidden], w_mlp2_bf, l, xf, mod3, 5, seq=seq, tm=512, tn=512)

    out = _rmsnorm(xf, final_norm_g.reshape(1, d))
    return out.reshape(batch, seq, d)
```

```python
import functools
import math

import numpy as np
import jax
import jax.numpy as jnp
from jax import lax
from jax.experimental import pallas as pl
from jax.experimental.pallas import tpu as pltpu

F32 = jnp.float32
BF16 = jnp.bfloat16

D_MODEL = 2048
HEAD_DIM = 128
N_HEADS = D_MODEL // HEAD_DIM
H_DIFF = N_HEADS // 4
H_FOX = (N_HEADS - H_DIFF) // 2
H_DIL = N_HEADS - H_FOX - H_DIFF
DIFF_QK_DIM = HEAD_DIM // 2
DIL_PATTERNS = ((128, 1), (512, 4), (2048, 16))
DIL_BLK = 128
DIL_UNROLL = 4
NUM_BUCKETS = 32
MAX_DISTANCE = 2048
D_FF = 4 * D_MODEL
N_MOD = 6
RMS_EPS = 1e-6
FOX_W = H_FOX * HEAD_DIM
DIL_W = H_DIL * HEAD_DIM
DIFF_W = H_DIFF * HEAD_DIM
NEG = -1e30

LANES = 128
SUBLANES = 8
VMEM_LIMIT_BYTES = 56 * 1024 * 1024


def _t5_thresholds():
    max_exact = NUM_BUCKETS // 2
    n = np.arange(max_exact, 2 * MAX_DISTANCE, dtype=np.float64)
    val = np.log(n / max_exact) / math.log(MAX_DISTANCE / max_exact) * (NUM_BUCKETS - max_exact)
    inside = n < MAX_DISTANCE
    frac = np.abs(val - np.round(val))
    assert np.all((frac[inside] > 1e-5) | (frac[inside] == 0.0))
    large = np.minimum(max_exact + np.floor(val).astype(np.int64), NUM_BUCKETS - 1)
    thr = list(range(max_exact))
    for b in range(max_exact, NUM_BUCKETS):
        thr.append(int(n[np.nonzero(large >= b)[0][0]]))
    return tuple(thr)


T5_THRESHOLDS = _t5_thresholds()


def _cparams(n_grid):
    return pltpu.CompilerParams(dimension_semantics=("arbitrary",) * n_grid,
                                vmem_limit_bytes=VMEM_LIMIT_BYTES)


def _dot(a, b):
    return jnp.dot(a, b, preferred_element_type=F32)


def _dot_nt(a, b):
    return lax.dot_general(a, b, (((1,), (1,)), ((), ())), preferred_element_type=F32)


def _ada_kernel(c_ref, w_ref, b_ref, o_ref):
    o_ref[...] = _dot(c_ref[...].astype(BF16), w_ref[...].astype(BF16)) + b_ref[...]


def _ada_mod(c, w_ada, b_ada, tn=512):
    depth, d, n = w_ada.shape
    b = c.shape[0]
    return pl.pallas_call(
        _ada_kernel,
        grid=(depth, n // tn),
        in_specs=[
            pl.BlockSpec((b, d), lambda l, j: (0, 0)),
            pl.BlockSpec((None, d, tn), lambda l, j: (l, 0, j)),
            pl.BlockSpec((None, 1, tn), lambda l, j: (l, 0, j)),
        ],
        out_specs=pl.BlockSpec((None, b, tn), lambda l, j: (l, 0, j)),
        out_shape=jax.ShapeDtypeStruct((depth, b, n), F32),
        compiler_params=_cparams(2),
        name="ada_mod",
    )(c, w_ada, b_ada.reshape(depth, 1, n))


def _t5_bias_of(n, table_ref, col):
    val = jnp.full(n.shape, table_ref[0, col], F32)
    for b in range(1, NUM_BUCKETS):
        val = jnp.where(n >= T5_THRESHOLDS[b], table_ref[b, col], val)
    return val


def _diff_bias_kernel(table_ref, o_ref, *, n_tiles):
    h = pl.program_id(0)
    row = lax.broadcasted_iota(jnp.int32, (LANES, LANES), 0)
    col = lax.broadcasted_iota(jnp.int32, (LANES, LANES), 1)

    def body(t, carry):
        o_ref[t] = _t5_bias_of(t * LANES + row - col, table_ref, H_DIL + h)
        return carry

    lax.fori_loop(0, n_tiles, body, 0)


def _diff_bias_tiles(rel_table, seq):
    n_tiles = seq // LANES
    return pl.pallas_call(
        functools.partial(_diff_bias_kernel, n_tiles=n_tiles),
        grid=(H_DIFF,),
        in_specs=[pl.BlockSpec(memory_space=pltpu.SMEM)],
        out_specs=pl.BlockSpec((None, n_tiles, LANES, LANES), lambda h: (h, 0, 0, 0)),
        out_shape=jax.ShapeDtypeStruct((H_DIFF, n_tiles, LANES, LANES), F32),
        compiler_params=_cparams(1),
        name="diff_bias_tiles",
    )(rel_table)


def _dil_bias_kernel(table_ref, o_ref):
    h = pl.program_id(0)
    qi = lax.broadcasted_iota(jnp.int32, (DIL_BLK, 2 * DIL_BLK), 0)
    ki = lax.broadcasted_iota(jnp.int32, (DIL_BLK, 2 * DIL_BLK), 1)
    dist = DIL_BLK + qi - ki
    valid = (dist >= 0) & (dist <= DIL_BLK)
    for p, (_, dil) in enumerate(DIL_PATTERNS):
        o_ref[p] = jnp.where(valid, _t5_bias_of(dist * dil, table_ref, h), NEG)


def _dil_bias_tiles(rel_table):
    n_pat = len(DIL_PATTERNS)
    return pl.pallas_call(
        _dil_bias_kernel,
        grid=(H_DIL,),
        in_specs=[pl.BlockSpec(memory_space=pltpu.SMEM)],
        out_specs=pl.BlockSpec((None, n_pat, DIL_BLK, 2 * DIL_BLK), lambda h: (h, 0, 0, 0)),
        out_shape=jax.ShapeDtypeStruct((H_DIL, n_pat, DIL_BLK, 2 * DIL_BLK), F32),
        compiler_params=_cparams(1),
        name="dil_bias_tiles",
    )(rel_table)


def _mod_norm(x, g, shift, scale):
    y = x * lax.rsqrt(jnp.mean(x * x, axis=-1, keepdims=True) + RMS_EPS)
    return (y * g) * (1.0 + scale) + shift


def _norm_mod_kernel(x_ref, g_ref, sh_ref, sc_ref, o_ref):
    o_ref[...] = _mod_norm(x_ref[...], g_ref[...], sh_ref[...], sc_ref[...]).astype(o_ref.dtype)


def _norm_mod(x, g, layer, mod3, shift_idx, scale_idx, *, seq, tm=512):
    m, d = x.shape
    tiles_per_seq = seq // tm
    return pl.pallas_call(
        _norm_mod_kernel,
        grid=(m // tm,),
        in_specs=[
            pl.BlockSpec((tm, d), lambda i: (i, 0)),
            pl.BlockSpec((None, 1, d), lambda i: (layer, 0, 0)),
            pl.BlockSpec((None, 1, d), lambda i: (i // tiles_per_seq, 0, shift_idx)),
            pl.BlockSpec((None, 1, d), lambda i: (i // tiles_per_seq, 0, scale_idx)),
        ],
        out_specs=pl.BlockSpec((tm, d), lambda i: (i, 0)),
        out_shape=jax.ShapeDtypeStruct((m, d), BF16),
        compiler_params=_cparams(1),
        name="norm_mod",
    )(x, g, mod3, mod3)


def _matmul_kernel(*refs, rhs_t, relu2, scaled):
    a_ref, w_ref = refs[:2]
    o_ref = refs[-1]
    acc = _dot_nt(a_ref[...], w_ref[...]) if rhs_t else _dot(a_ref[...], w_ref[...])
    if relu2:
        acc = jnp.square(jnp.maximum(acc, 0.0))
    if scaled:
        acc = acc * refs[2][...]
    o_ref[...] = acc.astype(o_ref.dtype)


def _matmul(a, w, layer, out_dtype, *, tm, tn, rhs_t, col_scale=None, relu2=False):
    m, k = a.shape
    n = w.shape[1] if rhs_t else w.shape[2]
    w_spec = (pl.BlockSpec((None, tn, k), lambda i, j: (layer, j, 0)) if rhs_t
              else pl.BlockSpec((None, k, tn), lambda i, j: (layer, 0, j)))
    in_specs = [pl.BlockSpec((tm, k), lambda i, j: (i, 0)), w_spec]
    operands = [a, w]
    if col_scale is not None:
        in_specs.append(pl.BlockSpec((1, tn), lambda i, j: (0, j)))
        operands.append(col_scale)
    return pl.pallas_call(
        functools.partial(_matmul_kernel, rhs_t=rhs_t, relu2=relu2, scaled=col_scale is not None),
        grid=(m // tm, n // tn),
        in_specs=in_specs,
        out_specs=pl.BlockSpec((tm, tn), lambda i, j: (i, j)),
        out_shape=jax.ShapeDtypeStruct((m, n), out_dtype),
        compiler_params=_cparams(2),
        name="matmul_relu2" if relu2 else "matmul",
    )(*operands)


def _gated_matmul_kernel(*refs, n_lhs, with_norm):
    lhs = refs[:n_lhs]
    ws = refs[n_lhs:2 * n_lhs]
    x_ref, gate_ref = refs[2 * n_lhs:2 * n_lhs + 2]
    acc = _dot(lhs[0][...], ws[0][...])
    for a_ref, w_ref in zip(lhs[1:], ws[1:]):
        acc = acc + _dot(a_ref[...], w_ref[...])
    out = x_ref[...] + gate_ref[...] * acc
    if with_norm:
        g_ref, sh_ref, sc_ref, o_ref, h_ref = refs[2 * n_lhs + 2:]
        h_ref[...] = _mod_norm(out, g_ref[...], sh_ref[...], sc_ref[...]).astype(h_ref.dtype)
    else:
        o_ref, = refs[2 * n_lhs + 2:]
    o_ref[...] = out


def _gated_matmul(lhs_list, w, layer, x, mod3, gate_idx, *, seq, tm, tn, norm=None):
    m, n = x.shape
    n_lhs = len(lhs_list)
    tiles_per_seq = seq // tm
    tn_per_d = n // tn
    in_specs, row0 = [], 0
    for a in lhs_list:
        in_specs.append(pl.BlockSpec((tm, a.shape[1]), lambda i, j: (i, 0)))
    for a in lhs_list:
        kp = a.shape[1]
        assert row0 % kp == 0
        in_specs.append(pl.BlockSpec((None, kp, tn), functools.partial(lambda i, j, r: (layer, r, j), r=row0 // kp)))
        row0 += kp
    assert row0 == w.shape[1]
    in_specs.append(pl.BlockSpec((tm, tn), lambda i, j: (i, j)))
    in_specs.append(pl.BlockSpec((None, 1, tn), lambda i, j: (i // tiles_per_seq, 0, gate_idx * tn_per_d + j)))
    operands = [*lhs_list, *([w] * n_lhs), x, mod3]
    out_specs = pl.BlockSpec((tm, tn), lambda i, j: (i, j))
    out_shape = jax.ShapeDtypeStruct((m, n), F32)
    if norm is not None:
        assert tn == n
        g, shift_idx, scale_idx = norm
        in_specs += [
            pl.BlockSpec((None, 1, n), lambda i, j: (layer, 0, 0)),
            pl.BlockSpec((None, 1, n), lambda i, j: (i // tiles_per_seq, 0, shift_idx)),
            pl.BlockSpec((None, 1, n), lambda i, j: (i // tiles_per_seq, 0, scale_idx)),
        ]
        operands += [g, mod3, mod3]
        out_specs = [out_specs, pl.BlockSpec((tm, tn), lambda i, j: (i, j))]
        out_shape = [out_shape, jax.ShapeDtypeStruct((m, n), BF16)]
    return pl.pallas_call(
        functools.partial(_gated_matmul_kernel, n_lhs=n_lhs, with_norm=norm is not None),
        grid=(m // tm, n // tn),
        in_specs=in_specs,
        out_specs=out_specs,
        out_shape=out_shape,
        compiler_params=_cparams(2),
        name="gated_matmul_norm" if norm is not None else "gated_matmul",
    )(*operands)


def _split3_bf16(x):
    hi = x.astype(BF16)
    r1 = x - hi.astype(F32)
    mid = r1.astype(BF16)
    lo = (r1 - mid.astype(F32)).astype(BF16)
    return hi, mid, lo


def _decay_cumsum_kernel(f_ref, b_ref, o_ref, *, n_blocks):
    row = lax.broadcasted_iota(jnp.int32, (LANES, LANES), 0)
    col = lax.broadcasted_iota(jnp.int32, (LANES, LANES), 1)
    tri = (col <= row).astype(BF16)

    def body(i, carry):
        rows = pl.ds(pl.multiple_of(i * LANES, LANES), LANES)
        z = f_ref[rows, :] + b_ref[...]
        log_f = jnp.minimum(z, 0.0) - jnp.log1p(jnp.exp(-jnp.abs(z)))
        hi, mid, lo = _split3_bf16(log_f)
        pre = (_dot(tri, hi) + _dot(tri, mid)) + _dot(tri, lo) + carry
        o_ref[rows, :] = pre
        return pre[LANES - 1:LANES, :]

    lax.fori_loop(0, n_blocks, body, jnp.zeros((1, LANES), F32))


def _decay_cumsum(proj_b, f_col_block, b_forget_pad, *, batch, seq):
    return pl.pallas_call(
        functools.partial(_decay_cumsum_kernel, n_blocks=seq // LANES),
        grid=(batch,),
        in_specs=[
            pl.BlockSpec((seq, LANES), lambda b: (b, f_col_block)),
            pl.BlockSpec((1, LANES), lambda b: (0, 0)),
        ],
        out_specs=pl.BlockSpec((None, seq, LANES), lambda b: (b, 0, 0)),
        out_shape=jax.ShapeDtypeStruct((batch, seq, LANES), F32),
        compiler_params=_cparams(1),
        name="decay_cumsum",
    )(proj_b, b_forget_pad)


HEADS_PER_STEP = 2


def _head_cols(u):
    return slice(u * HEAD_DIM, (u + 1) * HEAD_DIM)


def _flash_init(m_scr, l_scr, acc_scr):
    m_scr[...] = jnp.full(m_scr.shape, NEG, F32)
    l_scr[...] = jnp.zeros(l_scr.shape, F32)
    acc_scr[...] = jnp.zeros(acc_scr.shape, F32)


def _flash_step(logit_fns, value_fns, mask, m_scr, l_scr, acc_scr):
    scores = [fn() for fn in logit_fns]
    probs = []
    for u, s in enumerate(scores):
        if mask is not None:
            s = jnp.where(mask, s, NEG)
        m_prev = m_scr[u]
        m_new = jnp.maximum(m_prev, jnp.max(s, axis=-1, keepdims=True))
        alpha = jnp.exp(m_prev - m_new)
        p = jnp.exp(s - jnp.tile(m_new, (1, s.shape[1] // LANES)))
        l_scr[u] = alpha * l_scr[u] + jnp.sum(p, axis=-1, keepdims=True)
        m_scr[u] = m_new
        probs.append((alpha, p.astype(BF16)))
    for u, (alpha, p) in enumerate(probs):
        acc_scr[u] = alpha * acc_scr[u] + _dot(p, value_fns[u]())


def _causal_blocks(qi, tq, tk, step):
    n_full = (qi * tq) // tk

    def body(kb, carry):
        step(kb, None)
        return carry

    lax.fori_loop(0, n_full, body, 0)
    return n_full


N_SPLIT = 3


def _fox_kernel(q_ref, k_ref, v_ref, fcum_ref, o_ref, qaug_scr, kaug_scr, m_scr, l_scr, acc_scr, *, tq, tk, seq):
    hp = pl.program_id(1)
    qi = pl.program_id(2)

    @pl.when(qi == 0)
    def _():
        chunk = 512

        def build(c, carry):
            rows = pl.ds(pl.multiple_of(c * chunk, chunk), chunk)
            f = fcum_ref[rows, :]
            lane = lax.broadcasted_iota(jnp.int32, f.shape, 1)
            ones = jnp.where(lane < 2 * N_SPLIT, 1.0, 0.0)
            for u in range(HEADS_PER_STEP):
                f_h = jnp.sum(jnp.where(lane == hp * HEADS_PER_STEP + u, f, 0.0), axis=-1, keepdims=True)
                qa, ka = ones, ones
                for j, piece in enumerate(_split3_bf16(f_h)):
                    piece = piece.astype(F32)
                    qa = jnp.where(lane == j, piece, qa)
                    ka = jnp.where(lane == N_SPLIT + j, -piece, ka)
                qaug_scr[u, rows, :] = qa.astype(BF16)
                kaug_scr[u, rows, :] = ka.astype(BF16)
            return carry

        lax.fori_loop(0, seq // chunk, build, 0)

    _flash_init(m_scr, l_scr, acc_scr)
    q_rows = pl.ds(pl.multiple_of(qi * tq, tq), tq)

    def step(kb, mask):
        cols = pl.ds(pl.multiple_of(kb * tk, tk), tk)

        def logits(u):
            q2 = jnp.concatenate([q_ref[:, _head_cols(u)], qaug_scr[u, q_rows, :]], axis=1)
            k2 = jnp.concatenate([k_ref[cols, _head_cols(u)], kaug_scr[u, cols, :]], axis=1)
            return _dot_nt(q2, k2)

        heads = range(HEADS_PER_STEP)
        _flash_step([functools.partial(logits, u) for u in heads],
                    [functools.partial(lambda u: v_ref[cols, _head_cols(u)], u) for u in heads],
                    mask, m_scr, l_scr, acc_scr)

    n_full = _causal_blocks(qi, tq, tk, step)
    row = lax.broadcasted_iota(jnp.int32, (tq, tk), 0) + (qi * tq - n_full * tk)
    col = lax.broadcasted_iota(jnp.int32, (tq, tk), 1)
    step(n_full, col <= row)
    for u in range(HEADS_PER_STEP):
        o_ref[:, _head_cols(u)] = (acc_scr[u] / l_scr[u]).astype(o_ref.dtype)


def _fox_attention(proj, q_blk0, k_blk0, v_blk0, f_cum, *, batch, seq, tq=512, tk=512):
    nq = seq // tq
    hps = HEADS_PER_STEP
    wide = hps * HEAD_DIM
    assert q_blk0 % hps == 0 and k_blk0 % hps == 0 and v_blk0 % hps == 0 and H_FOX % hps == 0
    return pl.pallas_call(
        functools.partial(_fox_kernel, tq=tq, tk=tk, seq=seq),
        grid=(batch, H_FOX // hps, nq),
        in_specs=[
            pl.BlockSpec((tq, wide), lambda b, h, i: (b * nq + i, q_blk0 // hps + h)),
            pl.BlockSpec((seq, wide), lambda b, h, i: (b, k_blk0 // hps + h)),
            pl.BlockSpec((seq, wide), lambda b, h, i: (b, v_blk0 // hps + h)),
            pl.BlockSpec((None, seq, LANES), lambda b, h, i: (b, 0, 0)),
        ],
        out_specs=pl.BlockSpec((tq, wide), lambda b, h, i: (b * nq + i, h)),
        out_shape=jax.ShapeDtypeStruct((batch * seq, FOX_W), BF16),
        scratch_shapes=[pltpu.VMEM((hps, seq, HEAD_DIM), BF16), pltpu.VMEM((hps, seq, HEAD_DIM), BF16),
                        pltpu.VMEM((hps, tq, LANES), F32), pltpu.VMEM((hps, tq, LANES), F32),
                        pltpu.VMEM((hps, tq, HEAD_DIM), F32)],
        compiler_params=_cparams(3),
        name="fox_attention",
    )(proj, proj, proj, f_cum)


def _diff_kernel(q_ref, k_ref, v_ref, bias_ref, lam_ref, g_ref, o_ref, m_scr, l_scr, acc_scr,
                 *, tq, tk, lam_init):
    qi = pl.program_id(2)
    _flash_init(m_scr, l_scr, acc_scr)
    r_sub, c_sub = tq // LANES, tk // LANES

    def step(kb, mask):
        cols = pl.ds(pl.multiple_of(kb * tk, tk), tk)
        t0 = qi * r_sub - kb * c_sub

        def logits(u):
            q = q_ref[:, _head_cols(u)]
            lane = lax.broadcasted_iota(jnp.int32, q.shape, 1)
            zero = jnp.zeros_like(q)
            q2 = jnp.concatenate([jnp.where(lane < DIFF_QK_DIM, q, zero),
                                  jnp.where(lane >= DIFF_QK_DIM, q, zero)], axis=0)
            s = _dot_nt(q2, k_ref[cols, _head_cols(u)])
            bias = jnp.concatenate(
                [jnp.concatenate([bias_ref[u, jnp.maximum(t0 + a - c, 0)] for c in range(c_sub)], axis=1)
                 for a in range(r_sub)], axis=0)
            return s + jnp.concatenate([bias, bias], axis=0)

        heads = range(HEADS_PER_STEP)
        _flash_step([functools.partial(logits, u) for u in heads],
                    [functools.partial(lambda u: v_ref[cols, _head_cols(u)], u) for u in heads],
                    mask, m_scr, l_scr, acc_scr)

    n_full = _causal_blocks(qi, tq, tk, step)
    row = lax.broadcasted_iota(jnp.int32, (2 * tq, tk), 0)
    row = jnp.where(row >= tq, row - tq, row) + (qi * tq - n_full * tk)
    col = lax.broadcasted_iota(jnp.int32, (2 * tq, tk), 1)
    step(n_full, col <= row)

    lam_v = lam_ref[...]
    lam = (jnp.exp(jnp.sum(lam_v[0:1] * lam_v[1:2], axis=-1, keepdims=True))
           - jnp.exp(jnp.sum(lam_v[2:3] * lam_v[3:4], axis=-1, keepdims=True)) + lam_init)
    for u in range(HEADS_PER_STEP):
        o_all = acc_scr[u] / l_scr[u]
        o = o_all[:tq] - lam * o_all[tq:]
        y = o * lax.rsqrt(jnp.mean(o * o, axis=-1, keepdims=True) + RMS_EPS)
        o_ref[:, _head_cols(u)] = ((y * g_ref[...]) * (1.0 - lam_init)).astype(o_ref.dtype)


def _diff_attention(proj, q_blk0, k_blk0, v_blk0, bias_tiles, lam_rows, subln_g, *, batch, seq, lam_init,
                    tq=256, tk=512):
    nq = seq // tq
    n_tiles = seq // LANES
    hps = HEADS_PER_STEP
    wide = hps * HEAD_DIM
    assert q_blk0 % hps == 0 and k_blk0 % hps == 0 and v_blk0 % hps == 0 and H_DIFF % hps == 0
    return pl.pallas_call(
        functools.partial(_diff_kernel, tq=tq, tk=tk, lam_init=lam_init),
        grid=(batch, H_DIFF // hps, nq),
        in_specs=[
            pl.BlockSpec((tq, wide), lambda b, h, i: (b * nq + i, q_blk0 // hps + h)),
            pl.BlockSpec((seq, wide), lambda b, h, i: (b, k_blk0 // hps + h)),
            pl.BlockSpec((seq, wide), lambda b, h, i: (b, v_blk0 // hps + h)),
            pl.BlockSpec((hps, n_tiles, LANES, LANES), lambda b, h, i: (h, 0, 0, 0)),
            pl.BlockSpec((4, DIFF_QK_DIM), lambda b, h, i: (0, 0)),
            pl.BlockSpec((1, HEAD_DIM), lambda b, h, i: (0, 0)),
        ],
        out_specs=pl.BlockSpec((tq, wide), lambda b, h, i: (b * nq + i, h)),
        out_shape=jax.ShapeDtypeStruct((batch * seq, DIFF_W), BF16),
        scratch_shapes=[pltpu.VMEM((hps, 2 * tq, LANES), F32), pltpu.VMEM((hps, 2 * tq, LANES), F32),
                        pltpu.VMEM((hps, 2 * tq, HEAD_DIM), F32)],
        compiler_params=_cparams(3),
        name="diff_attention",
    )(proj, proj, proj, bias_tiles, lam_rows, subln_g)


def _dil_kernel(q_ref, k_ref, v_ref, bias_ref, o_ref, o_scr, lse_scr, *, seq):
    blk = DIL_BLK
    for p, (_, dil) in enumerate(DIL_PATTERNS):
        n_blocks = seq // (dil * blk)

        def body(it, carry, p=p, dil=dil, n_blocks=n_blocks):
            bias = bias_ref[p]
            scores = []
            for u in range(DIL_UNROLL):
                idx = it * DIL_UNROLL + u
                r = idx // n_blocks
                n = idx - r * n_blocks
                cur = pl.ds(r + n * (blk * dil), blk, stride=dil)
                prev = pl.ds(r + jnp.maximum(n - 1, 0) * (blk * dil), blk, stride=dil)
                q = q_ref[cur, :].astype(BF16)
                s_prev = _dot_nt(q, k_ref[prev, :].astype(BF16)) + bias[:, :blk]
                s_prev = jnp.where(n > 0, s_prev, NEG)
                s_cur = _dot_nt(q, k_ref[cur, :].astype(BF16)) + bias[:, blk:]
                scores.append((cur, prev, s_prev, s_cur))
            probs = []
            for cur, prev, s_prev, s_cur in scores:
                m = jnp.maximum(jnp.max(s_prev, axis=-1, keepdims=True), jnp.max(s_cur, axis=-1, keepdims=True))
                e_prev = jnp.exp(s_prev - m)
                e_cur = jnp.exp(s_cur - m)
                den = jnp.sum(e_prev, axis=-1, keepdims=True) + jnp.sum(e_cur, axis=-1, keepdims=True)
                inv = 1.0 / den
                probs.append((cur, prev, (e_prev * inv).astype(BF16), (e_cur * inv).astype(BF16), m + jnp.log(den)))
            for cur, prev, p_prev, p_cur, lse in probs:
                o = _dot(p_prev, v_ref[prev, :].astype(BF16)) + _dot(p_cur, v_ref[cur, :].astype(BF16))
                o_scr[p, cur, :] = o
                lse_scr[p, cur, :] = jnp.broadcast_to(lse, (blk, LANES))
            return carry

        lax.fori_loop(0, dil * n_blocks // DIL_UNROLL, body, 0)

    chunk = 512

    def mix(i, carry):
        rows = pl.ds(pl.multiple_of(i * chunk, chunk), chunk)
        lse = [lse_scr[p, rows, :] for p in range(len(DIL_PATTERNS))]
        mx = functools.reduce(jnp.maximum, lse)
        w = [jnp.exp(x - mx) for x in lse]
        inv = 1.0 / functools.reduce(lambda a, b: a + b, w)
        acc = (w[0] * inv) * o_scr[0, rows, :]
        for p in range(1, len(DIL_PATTERNS)):
            acc = acc + (w[p] * inv) * o_scr[p, rows, :]
        o_ref[rows, :] = acc.astype(o_ref.dtype)
        return carry

    lax.fori_loop(0, seq // chunk, mix, 0)


def _dil_attention(proj, q_blk0, k_blk0, v_blk0, bias_tiles, *, batch, seq):
    n_pat = len(DIL_PATTERNS)
    return pl.pallas_call(
        functools.partial(_dil_kernel, seq=seq),
        grid=(batch, H_DIL),
        in_specs=[
            pl.BlockSpec((seq, HEAD_DIM), lambda b, h: (b, q_blk0 + h)),
            pl.BlockSpec((seq, HEAD_DIM), lambda b, h: (b, k_blk0 + h)),
            pl.BlockSpec((seq, HEAD_DIM), lambda b, h: (b, v_blk0 + h)),
            pl.BlockSpec((None, n_pat, DIL_BLK, 2 * DIL_BLK), lambda b, h: (h, 0, 0, 0)),
        ],
        out_specs=pl.BlockSpec((seq, HEAD_DIM), lambda b, h: (b, h)),
        out_shape=jax.ShapeDtypeStruct((batch * seq, DIL_W), BF16),
        scratch_shapes=[pltpu.VMEM((n_pat, seq, HEAD_DIM), F32), pltpu.VMEM((n_pat, seq, LANES), F32)],
        compiler_params=_cparams(2),
        name="dil_attention",
    )(proj, proj, proj, bias_tiles)


def _rmsnorm_kernel(x_ref, g_ref, o_ref):
    x = x_ref[...]
    o_ref[...] = (x * lax.rsqrt(jnp.mean(x * x, axis=-1, keepdims=True) + RMS_EPS)) * g_ref[...]


def _rmsnorm(x, g, tm=512):
    m, d = x.shape
    return pl.pallas_call(
        _rmsnorm_kernel,
        grid=(m // tm,),
        in_specs=[pl.BlockSpec((tm, d), lambda i: (i, 0)), pl.BlockSpec((1, d), lambda i: (0, 0))],
        out_specs=pl.BlockSpec((tm, d), lambda i: (i, 0)),
        out_shape=jax.ShapeDtypeStruct((m, d), F32),
        compiler_params=_cparams(1),
        name="final_rmsnorm",
    )(x, g)


RELAYOUT_TN = 2 * LANES


def _relayout_kernel(src_ref, valid_ref, w_ref, o_ref):
    j = pl.program_id(0)
    row = lax.broadcasted_iota(jnp.int32, (RELAYOUT_TN, w_ref.shape[2]), 0)
    for l in range(w_ref.shape[1]):
        o_ref[l] = jnp.where(row < valid_ref[j], w_ref[:, l, :], 0.0).astype(o_ref.dtype)


def _relayout_w_in(w_t, tiles):
    n, depth, d = w_t.shape
    tn = RELAYOUT_TN
    assert all(0 <= c and c + tn <= n for c, _ in tiles)
    grid_spec = pltpu.PrefetchScalarGridSpec(
        num_scalar_prefetch=2,
        grid=(len(tiles),),
        in_specs=[pl.BlockSpec((pl.Element(tn), pl.Element(depth), pl.Element(d)),
                               lambda j, src, nv: (src[j], 0, 0))],
        out_specs=pl.BlockSpec((depth, tn, d), lambda j, src, nv: (0, j, 0)),
    )
    return pl.pallas_call(
        _relayout_kernel,
        grid_spec=grid_spec,
        out_shape=jax.ShapeDtypeStruct((depth, len(tiles) * tn, d), BF16),
        compiler_params=_cparams(1),
        name="relayout_w_in",
    )(jnp.asarray([c for c, _ in tiles], jnp.int32), jnp.asarray([v for _, v in tiles], jnp.int32), w_t)


def _split_w_in(w_in):
    fox_qkv = 3 * FOX_W
    dil_qkv = 3 * DIL_W
    o_dil = fox_qkv + H_FOX
    o_diff = o_dil + dil_qkv
    tn = RELAYOUT_TN
    tiles_a = ([(c, tn) for c in range(0, fox_qkv, tn)]
               + [(o_diff + c, tn) for c in range(0, 3 * DIFF_W, tn)])
    tiles_b = [(o_dil + c, tn) for c in range(0, dil_qkv, tn)] + [(fox_qkv, H_FOX)]
    w_t = jnp.transpose(w_in, (2, 0, 1))
    return _relayout_w_in(w_t, tiles_a), _relayout_w_in(w_t, tiles_b)


def kernel(x, c, w_ada, b_ada, norm1_g, w_in, b_forget, lambda_q1, lambda_k1, lambda_q2, lambda_k2,
           diff_subln_g, w_out, norm2_g, w_mlp1, w_mlp2, rel_table, final_norm_g):
    batch, seq, d = x.shape
    depth = w_ada.shape[0]
    assert d == D_MODEL and seq % (DIL_PATTERNS[-1][1] * DIL_BLK) == 0

    c_rows = jnp.zeros((SUBLANES, d), F32).at[:batch].set(c)
    mod = _ada_mod(c_rows, w_ada, b_ada)[:, :batch]
    diff_bias = _diff_bias_tiles(rel_table, seq)
    dil_bias = _dil_bias_tiles(rel_table)

    scale_a = jnp.ones((1, 3 * FOX_W + 3 * DIFF_W), F32)
    scale_a = scale_a.at[:, :FOX_W].set(HEAD_DIM ** -0.5)
    scale_a = scale_a.at[:, 3 * FOX_W:3 * FOX_W + DIFF_W].set(DIFF_QK_DIM ** -0.5)
    scale_b = jnp.ones((1, 3 * DIL_W + 2 * LANES), F32).at[:, :DIL_W].set(HEAD_DIM ** -0.5)

    w_a, w_b = _split_w_in(w_in)
    w_out_bf, w_mlp1_bf, w_mlp2_bf = w_out.astype(BF16), w_mlp1.astype(BF16), w_mlp2.astype(BF16)
    g1 = norm1_g.reshape(depth, 1, d)
    g2 = norm2_g.reshape(depth, 1, d)

    xf = x.reshape(batch * seq, d)
    for l in range(depth):
        mod3 = mod[l].reshape(batch, 1, N_MOD * d)
        h = _norm_mod(xf, g1, l, mod3, 0, 1, seq=seq)
        proj_a = _matmul(h, w_a, l, BF16, tm=1024, tn=768, rhs_t=True, col_scale=scale_a)
        proj_b = _matmul(h, w_b, l, F32, tm=1024, tn=512, rhs_t=True, col_scale=scale_b)

        b_f = jnp.zeros((1, LANES), F32).at[0, :H_FOX].set(b_forget[l])
        f_cum = _decay_cumsum(proj_b, 3 * DIL_W // LANES, b_f, batch=batch, seq=seq)
        y_a = _fox_attention(proj_a, 0, H_FOX, 2 * H_FOX, f_cum, batch=batch, seq=seq)

        y_b = _dil_attention(proj_b, 0, H_DIL, 2 * H_DIL, dil_bias, batch=batch, seq=seq)

        lam_init = 0.8 - 0.6 * math.exp(-0.3 * l)
        lam_rows = jnp.stack([lambda_q1[l], lambda_k1[l], lambda_q2[l], lambda_k2[l]]).astype(F32)
        c0 = 3 * H_FOX
        y_c = _diff_attention(proj_a, c0, c0 + H_DIFF, c0 + 2 * H_DIFF, diff_bias, lam_rows,
                              diff_subln_g[l].reshape(1, HEAD_DIM), batch=batch, seq=seq, lam_init=lam_init)

        xf, h2 = _gated_matmul([y_a, y_b, y_c], w_out_bf, l, xf, mod3, 2, seq=seq, tm=512, tn=d,
                               norm=(g2, 3, 4))

        hidden = _matmul(h2, w_mlp1_bf, l, BF16, tm=1024, tn=1024, rhs_t=False, relu2=True)
        xf = _gated_matmul([hidden], w_mlp2_bf, l, xf, mod3, 5, seq=seq, tm=512, tn=512)

    out = _rmsnorm(xf, final_norm_g.reshape(1, d))
    return out.reshape(batch, seq, d)
```
